```python
import jax, jax.numpy as jnp
from jax import lax
import numpy as np

D_MODEL = 2048
BATCH = 4
SEQ = 2048
DEPTH = 4

CONF_WIDTH = D_MODEL // 2
CONF_KERNEL = 31
SCONV_WIDTH = D_MODEL // 2
SCONV_KERNEL = 3
HEAD_DIM = 64
N_Q_HEADS = (D_MODEL // 2) // HEAD_DIM
N_KV_HEADS = 4
Q_WIDTH = N_Q_HEADS * HEAD_DIM
KV_WIDTH = N_KV_HEADS * HEAD_DIM
WINDOW = 128
BLOCK = 128
N_BRANCH = 3
D_FF = 4 * D_MODEL
RMS_EPS = 1e-6
LN_EPS = 1e-5
IN_SPLITS = (CONF_WIDTH, CONF_WIDTH,
             SCONV_WIDTH, SCONV_WIDTH, SCONV_WIDTH,
             Q_WIDTH, KV_WIDTH, KV_WIDTH,
             N_BRANCH * D_MODEL)
IN_WIDTH = sum(IN_SPLITS)

kernel_name = "hybrid_conformer_shortconv_swa_sink_block"


def rmsnorm(x, g):
    xf = x.astype(jnp.float32)
    y = xf * lax.rsqrt(jnp.mean(xf * xf, axis=-1, keepdims=True) + RMS_EPS)
    return y.astype(x.dtype) * g


def layernorm(x, g, b):
    xf = x.astype(jnp.float32)
    mu = jnp.mean(xf, axis=-1, keepdims=True)
    var = jnp.mean(jnp.square(xf - mu), axis=-1, keepdims=True)
    y = (xf - mu) * lax.rsqrt(var + LN_EPS)
    return y.astype(x.dtype) * g + b


def causal_depthwise_conv(x, w):
    k, c = w.shape
    return lax.conv_general_dilated(
        x, w[:, None, :].astype(x.dtype), window_strides=(1,),
        padding=((k - 1, 0),), dimension_numbers=("NWC", "WIO", "NWC"),
        feature_group_count=c)


def split_columns(u):
    outs, start = [], 0
    for width in IN_SPLITS:
        outs.append(u[..., start:start + width])
        start += width
    return outs


def sliding_window_sink_attention(q, k, v, sinks):
    b, s, _ = q.shape
    nb = s // BLOCK
    grp = N_Q_HEADS // N_KV_HEADS
    qb = q.reshape(b, nb, BLOCK, N_KV_HEADS, grp, HEAD_DIM)

    def band(t):
        t = t.reshape(b, s, N_KV_HEADS, HEAD_DIM)
        prev = jnp.pad(t, ((0, 0), (BLOCK, 0), (0, 0), (0, 0)))[:, :s]
        prev = prev.reshape(b, nb, BLOCK, N_KV_HEADS, HEAD_DIM)
        cur = t.reshape(b, nb, BLOCK, N_KV_HEADS, HEAD_DIM)
        return jnp.concatenate([prev, cur], axis=2)

    kb, vb = band(k), band(v)
    scores = jnp.einsum("bnqhgd,bnkhd->bnhgqk", qb, kb,
                        preferred_element_type=jnp.float32) * (HEAD_DIM ** -0.5)
    blk = jnp.arange(nb)
    qpos = blk[:, None, None] * BLOCK + jnp.arange(BLOCK)[None, :, None]
    kpos = (blk[:, None, None] - 1) * BLOCK + jnp.arange(2 * BLOCK)[None, None, :]
    diff = qpos - kpos
    valid = (diff >= 0) & (diff < WINDOW) & (kpos >= 0)
    scores = jnp.where(valid[None, :, None, None], scores, -jnp.inf)
    sink = sinks.astype(jnp.float32).reshape(1, 1, N_KV_HEADS, grp, 1, 1)
    m = jnp.maximum(jnp.max(scores, axis=-1, keepdims=True), sink)
    p = jnp.exp(scores - m)
    denom = jnp.sum(p, axis=-1, keepdims=True) + jnp.exp(sink - m)
    probs = (p / denom).astype(v.dtype)
    out = jnp.einsum("bnhgqk,bnkhd->bnqhgd", probs, vb)
    return out.reshape(b, s, Q_WIDTH)


def setup_inputs(seed: int = 0) -> dict:
    key = jax.random.key(seed)
    ks = jax.random.split(key, 20)
    f32 = jnp.float32

    def nrm(k, shape, scale):
        return jax.random.normal(k, shape, f32) * scale

    return {
        "x": nrm(ks[0], (BATCH, SEQ, D_MODEL), 1.0),
        "norm_mix_g": 1.0 + nrm(ks[1], (DEPTH, D_MODEL), 0.02),
        "w_in": nrm(ks[2], (DEPTH, D_MODEL, IN_WIDTH), D_MODEL ** -0.5),
        "gate_b": nrm(ks[3], (DEPTH, N_BRANCH, D_MODEL), 0.02),
        "conf_dw": nrm(ks[4], (DEPTH, CONF_KERNEL, CONF_WIDTH), CONF_KERNEL ** -0.5),
        "conf_ln_g": 1.0 + nrm(ks[5], (DEPTH, CONF_WIDTH), 0.02),
        "conf_ln_b": nrm(ks[6], (DEPTH, CONF_WIDTH), 0.02),
        "w_conf_out": nrm(ks[7], (DEPTH, CONF_WIDTH, D_MODEL), CONF_WIDTH ** -0.5),
        "sconv_w": nrm(ks[8], (DEPTH, SCONV_KERNEL, SCONV_WIDTH), SCONV_KERNEL ** -0.5),
        "w_sconv_out": nrm(ks[9], (DEPTH, SCONV_WIDTH, D_MODEL), SCONV_WIDTH ** -0.5),
        "sinks": nrm(ks[10], (DEPTH, N_Q_HEADS), 0.5),
        "w_attn_out": nrm(ks[11], (DEPTH, Q_WIDTH, D_MODEL), Q_WIDTH ** -0.5),
        "w_mix_out": nrm(ks[12], (DEPTH, D_MODEL, D_MODEL), D_MODEL ** -0.5),
        "norm_ffn_g": 1.0 + nrm(ks[13], (DEPTH, D_MODEL), 0.02),
        "w_up": nrm(ks[14], (DEPTH, D_MODEL, D_FF), D_MODEL ** -0.5),
        "w_down": nrm(ks[15], (DEPTH, D_FF, D_MODEL), D_FF ** -0.5),
        "final_g": 1.0 + nrm(ks[16], (D_MODEL,), 0.02),
    }


def reference(x, norm_mix_g, w_in, gate_b, conf_dw, conf_ln_g, conf_ln_b, w_conf_out,
              sconv_w, w_sconv_out, sinks, w_attn_out, w_mix_out, norm_ffn_g,
              w_up, w_down, final_g):
    b, s, d = x.shape
    for l in range(DEPTH):
        h = rmsnorm(x, norm_mix_g[l])
        u = h @ w_in[l]
        a_val, a_gate, b_gate, c_gate, b_h, q, k, v, g = split_columns(u)

        a = a_val * jax.nn.sigmoid(a_gate)
        a = causal_depthwise_conv(a, conf_dw[l])
        a = jax.nn.silu(layernorm(a, conf_ln_g[l], conf_ln_b[l]))
        y_a = a @ w_conf_out[l]

        sc = causal_depthwise_conv(c_gate * b_h, sconv_w[l])
        y_b = (b_gate * sc) @ w_sconv_out[l]

        y_c = sliding_window_sink_attention(q, k, v, sinks[l]) @ w_attn_out[l]

        gates = jax.nn.sigmoid(g.reshape(b, s, N_BRANCH, d) + gate_b[l])
        merged = gates[:, :, 0] * y_a + gates[:, :, 1] * y_b + gates[:, :, 2] * y_c
        x = x + merged @ w_mix_out[l]

        h = rmsnorm(x, norm_ffn_g[l])
        x = x + jnp.square(jax.nn.relu(h @ w_up[l])) @ w_down[l]
    return rmsnorm(x, final_g)
```

```python
import functools

import jax
import jax.numpy as jnp
from jax import lax
from jax.experimental import pallas as pl
from jax.experimental.pallas import tpu as pltpu

D_MODEL = 2048
CONF_WIDTH = 1024
CONF_KERNEL = 31
SCONV_WIDTH = 1024
SCONV_KERNEL = 3
HEAD_DIM = 64
N_Q_HEADS = 16
N_KV_HEADS = 4
GROUP = N_Q_HEADS // N_KV_HEADS
Q_WIDTH = N_Q_HEADS * HEAD_DIM
KV_WIDTH = N_KV_HEADS * HEAD_DIM
WINDOW = 128
BLOCK = 128
N_BRANCH = 3
D_FF = 4 * D_MODEL
RMS_EPS = 1e-6
LN_EPS = 1e-5

OFF_A_VAL = 0
OFF_A_GATE = OFF_A_VAL + CONF_WIDTH
OFF_B_GATE = OFF_A_GATE + CONF_WIDTH
OFF_C_GATE = OFF_B_GATE + SCONV_WIDTH
OFF_B_H = OFF_C_GATE + SCONV_WIDTH
OFF_Q = OFF_B_H + SCONV_WIDTH
OFF_K = OFF_Q + Q_WIDTH
OFF_V = OFF_K + KV_WIDTH
OFF_G = OFF_V + KV_WIDTH
IN_WIDTH = OFF_G + N_BRANCH * D_MODEL

CONF_HALO = 32
SCONV_HALO = 16
VMEM_LIMIT = 56 * 1024 * 1024


def _params(sem):
    return pltpu.CompilerParams(dimension_semantics=sem, vmem_limit_bytes=VMEM_LIMIT)


def _rmsnorm_rows(x, g):
    ms = jnp.mean(x * x, axis=-1, keepdims=True)
    return x * lax.rsqrt(ms + RMS_EPS) * g


def _rmsnorm_kernel(x_ref, g_ref, o_ref):
    o_ref[...] = _rmsnorm_rows(x_ref[...], g_ref[...]).astype(o_ref.dtype)


def _rmsnorm(x, g, out_dtype, tm=512):
    t, d = x.shape
    return pl.pallas_call(
        _rmsnorm_kernel,
        grid=(t // tm,),
        in_specs=[pl.BlockSpec((tm, d), lambda i: (i, 0)),
                  pl.BlockSpec((1, d), lambda i: (0, 0))],
        out_specs=pl.BlockSpec((tm, d), lambda i: (i, 0)),
        out_shape=jax.ShapeDtypeStruct((t, d), out_dtype),
        compiler_params=_params(("parallel",)),
        name="rmsnorm",
    )(x, g.reshape(1, d))


def _mm_kernel(a_ref, w_ref, o_ref, *, relu2):
    acc = jnp.dot(a_ref[...], w_ref[...], preferred_element_type=jnp.float32)
    if relu2:
        acc = jnp.square(jnp.maximum(acc, 0.0))
    o_ref[...] = acc.astype(o_ref.dtype)


def _matmul(a, w, *, tm, tn, relu2=False, name):
    m, k = a.shape
    _, n = w.shape
    return pl.pallas_call(
        functools.partial(_mm_kernel, relu2=relu2),
        grid=(m // tm, n // tn),
        in_specs=[pl.BlockSpec((tm, k), lambda i, j: (i, 0)),
                  pl.BlockSpec((k, tn), lambda i, j: (0, j))],
        out_specs=pl.BlockSpec((tm, tn), lambda i, j: (i, j)),
        out_shape=jax.ShapeDtypeStruct((m, n), jnp.bfloat16),
        compiler_params=_params(("parallel", "parallel")),
        name=name,
    )(a, w)


def _sigmoid(x):
    return 1.0 / (1.0 + jnp.exp(-x))


def _mixer_kernel(sinks_ref,
                  av_ref, ag_ref, bg_ref, cg_ref, bh_ref, q_ref, k_ref, v_ref,
                  hav_ref, hag_ref, hcg_ref, hbh_ref, hk_ref, hv_ref,
                  cw_ref, lng_ref, lnb_ref, sw_ref,
                  a_out, b_out, c_out,
                  glu_scr, conv_scr, cb_scr, kb_scr, vb_scr,
                  *, tm, tiles_per_seq):
    f32 = jnp.float32
    first = (pl.program_id(0) % tiles_per_seq) == 0
    keep = jnp.where(first, 0.0, 1.0).astype(f32)

    hal = BLOCK - CONF_HALO
    halo_glu = hav_ref[hal:, :].astype(f32) * _sigmoid(hag_ref[hal:, :].astype(f32))
    glu_scr[0:CONF_HALO, :] = halo_glu * keep
    glu_scr[CONF_HALO:, :] = av_ref[...].astype(f32) * _sigmoid(ag_ref[...].astype(f32))

    hbl = BLOCK - SCONV_HALO
    cb_scr[0:SCONV_HALO, :] = hcg_ref[hbl:, :].astype(f32) * hbh_ref[hbl:, :].astype(f32) * keep
    cb_scr[SCONV_HALO:, :] = cg_ref[...].astype(f32) * bh_ref[...].astype(f32)

    rows = 128
    def conv_body(c, carry):
        lanes = pl.ds(pl.multiple_of(c * 128, 128), 128)
        for r in range(tm // rows):
            acc = jnp.zeros((rows, 128), f32)
            for j in range(CONF_KERNEL):
                start = r * rows + CONF_HALO - (CONF_KERNEL - 1) + j
                acc = acc + cw_ref[pl.ds(j, 1), lanes] * glu_scr[pl.ds(start, rows), lanes]
            conv_scr[pl.ds(r * rows, rows), lanes] = acc
            sc = jnp.zeros((rows, 128), f32)
            for j in range(SCONV_KERNEL):
                start = r * rows + SCONV_HALO - (SCONV_KERNEL - 1) + j
                sc = sc + sw_ref[pl.ds(j, 1), lanes] * cb_scr[pl.ds(start, rows), lanes]
            bg = bg_ref[pl.ds(r * rows, rows), lanes].astype(f32)
            b_out[pl.ds(r * rows, rows), lanes] = (bg * sc).astype(b_out.dtype)
        return carry
    lax.fori_loop(0, CONF_WIDTH // 128, conv_body, 0)

    def ln_body(r, carry):
        rs = pl.ds(pl.multiple_of(r * rows, rows), rows)
        y = conv_scr[rs, :]
        mu = jnp.mean(y, axis=-1, keepdims=True)
        yc = y - mu
        var = jnp.mean(yc * yc, axis=-1, keepdims=True)
        z = yc * lax.rsqrt(var + LN_EPS) * lng_ref[...] + lnb_ref[...]
        a_out[rs, :] = (z * _sigmoid(z)).astype(a_out.dtype)
        return carry
    lax.fori_loop(0, tm // rows, ln_body, 0)

    kb_scr[0:BLOCK, :] = (hk_ref[...].astype(f32) * keep).astype(kb_scr.dtype)
    kb_scr[BLOCK:, :] = k_ref[...]
    vb_scr[0:BLOCK, :] = (hv_ref[...].astype(f32) * keep).astype(vb_scr.dtype)
    vb_scr[BLOCK:, :] = v_ref[...]

    qi = lax.broadcasted_iota(jnp.int32, (BLOCK, 2 * BLOCK), 0)
    kj = lax.broadcasted_iota(jnp.int32, (BLOCK, 2 * BLOCK), 1)
    band = (kj > qi) & (kj <= qi + WINDOW)

    def attn_body(b, carry):
        qs = pl.ds(pl.multiple_of(b * BLOCK, BLOCK), BLOCK)
        ks = pl.ds(pl.multiple_of(b * BLOCK, BLOCK), 2 * BLOCK)
        has_prev = jnp.logical_or(jnp.logical_not(first), b > 0)
        valid = band & ((kj >= BLOCK) | has_prev)
        kband = kb_scr[ks, :]
        vband = vb_scr[ks, :]
        qblk = q_ref[qs, :]
        outs = []
        for h in range(N_KV_HEADS):
            kh = kband[:, h * HEAD_DIM:(h + 1) * HEAD_DIM]
            vh = vband[:, h * HEAD_DIM:(h + 1) * HEAD_DIM]
            for g in range(GROUP):
                hq = h * GROUP + g
                qh = qblk[:, hq * HEAD_DIM:(hq + 1) * HEAD_DIM]
                s = lax.dot_general(qh, kh, (((1,), (1,)), ((), ())),
                                    preferred_element_type=f32) * (HEAD_DIM ** -0.5)
                s = jnp.where(valid, s, -jnp.inf)
                sink = sinks_ref[hq]
                m = jnp.maximum(jnp.max(s, axis=-1, keepdims=True), sink)
                p = jnp.exp(s - m)
                denom = jnp.sum(p, axis=-1, keepdims=True) + jnp.exp(sink - m)
                probs = (p / denom).astype(vh.dtype)
                outs.append(jnp.dot(probs, vh, preferred_element_type=f32))
        c_out[qs, :] = jnp.concatenate(outs, axis=-1).astype(c_out.dtype)
        return carry
    lax.fori_loop(0, tm // BLOCK, attn_body, 0)


def _mixer(u, sinks, conf_dw, ln_g, ln_b, sconv_w, *, seq, tm=512):
    t = u.shape[0]
    tiles_per_seq = seq // tm
    bpt = tm // BLOCK

    def col(width, off):
        return pl.BlockSpec((tm, width), lambda i: (i, off // width))

    def halo(width, off):
        return pl.BlockSpec((BLOCK, width), lambda i: (jnp.maximum(i * bpt - 1, 0), off // width))

    def whole(shape):
        return pl.BlockSpec(shape, lambda i: (0,) * len(shape))

    cw, kw = CONF_WIDTH, KV_WIDTH
    in_specs = [
        pl.BlockSpec(memory_space=pltpu.SMEM),
        col(cw, OFF_A_VAL), col(cw, OFF_A_GATE), col(cw, OFF_B_GATE), col(cw, OFF_C_GATE),
        col(cw, OFF_B_H), col(cw, OFF_Q), col(kw, OFF_K), col(kw, OFF_V),
        halo(cw, OFF_A_VAL), halo(cw, OFF_A_GATE), halo(cw, OFF_C_GATE), halo(cw, OFF_B_H),
        halo(kw, OFF_K), halo(kw, OFF_V),
        whole((CONF_KERNEL, cw)), whole((1, cw)), whole((1, cw)), whole((SCONV_KERNEL, cw)),
    ]
    out_spec = pl.BlockSpec((tm, cw), lambda i: (i, 0))
    out_sds = jax.ShapeDtypeStruct((t, cw), jnp.bfloat16)
    return pl.pallas_call(
        functools.partial(_mixer_kernel, tm=tm, tiles_per_seq=tiles_per_seq),
        grid=(t // tm,),
        in_specs=in_specs,
        out_specs=[out_spec, out_spec, out_spec],
        out_shape=[out_sds, out_sds, out_sds],
        scratch_shapes=[
            pltpu.VMEM((tm + CONF_HALO, cw), jnp.float32),
            pltpu.VMEM((tm, cw), jnp.float32),
            pltpu.VMEM((tm + SCONV_HALO, cw), jnp.float32),
            pltpu.VMEM((tm + BLOCK, kw), jnp.bfloat16),
            pltpu.VMEM((tm + BLOCK, kw), jnp.bfloat16),
        ],
        compiler_params=_params(("parallel",)),
        name="mixer",
    )(sinks, u, u, u, u, u, u, u, u, u, u, u, u, u, u,
      conf_dw, ln_g.reshape(1, cw), ln_b.reshape(1, cw), sconv_w)


def _gated_proj_kernel(a_ref, b_ref, c_ref, wa_ref, wb_ref, wc_ref,
                       g0_ref, g1_ref, g2_ref, gb_ref, o_ref):
    f32 = jnp.float32
    ya = jnp.dot(a_ref[...], wa_ref[...], preferred_element_type=f32)
    yb = jnp.dot(b_ref[...], wb_ref[...], preferred_element_type=f32)
    yc = jnp.dot(c_ref[...], wc_ref[...], preferred_element_type=f32)
    g0 = _sigmoid(g0_ref[...].astype(f32) + gb_ref[0:1, :])
    g1 = _sigmoid(g1_ref[...].astype(f32) + gb_ref[1:2, :])
    g2 = _sigmoid(g2_ref[...].astype(f32) + gb_ref[2:3, :])
    o_ref[...] = (g0 * ya + g1 * yb + g2 * yc).astype(o_ref.dtype)


def _gated_proj(a, b, c, wa, wb, wc, u, gate_b, *, tm=1024, tn=512):
    t = a.shape[0]
    kdim = a.shape[1]
    g_blk0 = OFF_G // tn
    per_gate = D_MODEL // tn
    act = pl.BlockSpec((tm, kdim), lambda i, j: (i, 0))
    wsp = pl.BlockSpec((kdim, tn), lambda i, j: (0, j))

    def gate(bi):
        return pl.BlockSpec((tm, tn), lambda i, j: (i, g_blk0 + bi * per_gate + j))

    return pl.pallas_call(
        _gated_proj_kernel,
        grid=(t // tm, D_MODEL // tn),
        in_specs=[act, act, act, wsp, wsp, wsp, gate(0), gate(1), gate(2),
                  pl.BlockSpec((N_BRANCH, tn), lambda i, j: (0, j))],
        out_specs=pl.BlockSpec((tm, tn), lambda i, j: (i, j)),
        out_shape=jax.ShapeDtypeStruct((t, D_MODEL), jnp.bfloat16),
        compiler_params=_params(("parallel", "parallel")),
        name="gated_proj",
    )(a, b, c, wa, wb, wc, u, u, u, gate_b)


def _mm_res_norm_kernel(a_ref, w_ref, x_ref, g_ref, xo_ref, ho_ref):
    k = pl.program_id(1)

    @pl.when(k == 0)
    def _():
        xo_ref[...] = x_ref[...]

    xo_ref[...] += jnp.dot(a_ref[...], w_ref[...], preferred_element_type=jnp.float32)

    @pl.when(k == pl.num_programs(1) - 1)
    def _():
        ho_ref[...] = _rmsnorm_rows(xo_ref[...], g_ref[...]).astype(ho_ref.dtype)


def _mm_res_norm(a, w, x, g, *, tm, tk, norm_dtype, name):
    m, kdim = a.shape
    n = w.shape[1]
    return pl.pallas_call(
        _mm_res_norm_kernel,
        grid=(m // tm, kdim // tk),
        in_specs=[pl.BlockSpec((tm, tk), lambda i, k: (i, k)),
                  pl.BlockSpec((tk, n), lambda i, k: (k, 0)),
                  pl.BlockSpec((tm, n), lambda i, k: (i, 0)),
                  pl.BlockSpec((1, n), lambda i, k: (0, 0))],
        out_specs=[pl.BlockSpec((tm, n), lambda i, k: (i, 0)),
                   pl.BlockSpec((tm, n), lambda i, k: (i, 0))],
        out_shape=[jax.ShapeDtypeStruct((m, n), jnp.float32),
                   jax.ShapeDtypeStruct((m, n), norm_dtype)],
        compiler_params=_params(("parallel", "arbitrary")),
        name=name,
    )(a, w, x, g.reshape(1, n))


def kernel(x, norm_mix_g, w_in, gate_b, conf_dw, conf_ln_g, conf_ln_b, w_conf_out,
           sconv_w, w_sconv_out, sinks, w_attn_out, w_mix_out, norm_ffn_g,
           w_up, w_down, final_g):
    bsz, seq, d = x.shape
    depth = w_in.shape[0]
    t = bsz * seq
    bf16 = jnp.bfloat16
    xf = x.reshape(t, d)

    h = _rmsnorm(xf, norm_mix_g[0], bf16)
    for l in range(depth):
        u = _matmul(h, w_in[l].astype(bf16), tm=1024, tn=1280, name="in_proj")
        a_act, b_act, c_act = _mixer(u, sinks[l], conf_dw[l], conf_ln_g[l], conf_ln_b[l],
                                     sconv_w[l], seq=seq)
        merged = _gated_proj(a_act, b_act, c_act,
                             w_conf_out[l].astype(bf16), w_sconv_out[l].astype(bf16),
                             w_attn_out[l].astype(bf16), u, gate_b[l])
        xf, h2 = _mm_res_norm(merged, w_mix_out[l].astype(bf16), xf, norm_ffn_g[l],
                              tm=512, tk=D_MODEL, norm_dtype=bf16, name="mix_out")
        hmid = _matmul(h2, w_up[l].astype(bf16), tm=1024, tn=1024, relu2=True, name="mlp_up")
        last = l == depth - 1
        g_next = final_g if last else norm_mix_g[l + 1]
        xf, h = _mm_res_norm(hmid, w_down[l].astype(bf16), xf, g_next,
                             tm=512, tk=2048, norm_dtype=jnp.float32 if last else bf16,
                             name="mlp_down")
    return h.reshape(bsz, seq, d)
```

```python
import functools

import jax
import jax.numpy as jnp
from jax import lax
from jax.experimental import pallas as pl
from jax.experimental.pallas import tpu as pltpu

D_MODEL = 2048
CONF_WIDTH = 1024
CONF_KERNEL = 31
SCONV_WIDTH = 1024
SCONV_KERNEL = 3
HEAD_DIM = 64
N_Q_HEADS = 16
N_KV_HEADS = 4
GROUP = N_Q_HEADS // N_KV_HEADS
Q_WIDTH = N_Q_HEADS * HEAD_DIM
KV_WIDTH = N_KV_HEADS * HEAD_DIM
WINDOW = 128
BLOCK = 128
N_BRANCH = 3
D_FF = 4 * D_MODEL
RMS_EPS = 1e-6
LN_EPS = 1e-5

OFF_A_VAL = 0
OFF_A_GATE = OFF_A_VAL + CONF_WIDTH
OFF_B_GATE = OFF_A_GATE + CONF_WIDTH
OFF_C_GATE = OFF_B_GATE + SCONV_WIDTH
OFF_B_H = OFF_C_GATE + SCONV_WIDTH
OFF_Q = OFF_B_H + SCONV_WIDTH
OFF_K = OFF_Q + Q_WIDTH
OFF_V = OFF_K + KV_WIDTH
OFF_G = OFF_V + KV_WIDTH
IN_WIDTH = OFF_G + N_BRANCH * D_MODEL
assert GROUP * HEAD_DIM == 2 * BLOCK == KV_WIDTH

CONF_HALO = 32
SCONV_HALO = 16
VMEM_LIMIT = 56 * 1024 * 1024


def _params(sem):
    return pltpu.CompilerParams(dimension_semantics=sem, vmem_limit_bytes=VMEM_LIMIT)


def _rmsnorm_rows(x, g):
    ms = jnp.mean(x * x, axis=-1, keepdims=True)
    return x * lax.rsqrt(ms + RMS_EPS) * g


def _rmsnorm_kernel(x_ref, g_ref, o_ref):
    o_ref[...] = _rmsnorm_rows(x_ref[...], g_ref[...]).astype(o_ref.dtype)


def _rmsnorm(x, g, out_dtype, tm=512):
    t, d = x.shape
    return pl.pallas_call(
        _rmsnorm_kernel,
        grid=(t // tm,),
        in_specs=[pl.BlockSpec((tm, d), lambda i: (i, 0)),
                  pl.BlockSpec((1, d), lambda i: (0, 0))],
        out_specs=pl.BlockSpec((tm, d), lambda i: (i, 0)),
        out_shape=jax.ShapeDtypeStruct((t, d), out_dtype),
        compiler_params=_params(("parallel",)),
        name="rmsnorm",
    )(x, g.reshape(1, d))


CAST_ROWS = 256


def _mm_wcast_kernel(a_ref, w_ref, o_ref, wb_ref, *, relu2):
    @pl.when(pl.program_id(1) == 0)
    def _():
        def cast_rows(r, carry):
            rs = pl.ds(pl.multiple_of(r * CAST_ROWS, CAST_ROWS), CAST_ROWS)
            wb_ref[rs, :] = w_ref[rs, :].astype(wb_ref.dtype)
            return carry
        lax.fori_loop(0, w_ref.shape[0] // CAST_ROWS, cast_rows, 0)

    acc = jnp.dot(a_ref[...], wb_ref[...], preferred_element_type=jnp.float32)
    if relu2:
        acc = jnp.square(jnp.maximum(acc, 0.0))
    o_ref[...] = acc.astype(o_ref.dtype)


def _matmul_wcast(a, w_stack, layer, *, tm, tn, relu2=False, name):
    m, k = a.shape
    n = w_stack.shape[2]
    return pl.pallas_call(
        functools.partial(_mm_wcast_kernel, relu2=relu2),
        grid=(n // tn, m // tm),
        in_specs=[pl.BlockSpec((tm, k), lambda j, i: (i, 0)),
                  pl.BlockSpec((None, k, tn), lambda j, i: (layer, 0, j))],
        out_specs=pl.BlockSpec((tm, tn), lambda j, i: (i, j)),
        out_shape=jax.ShapeDtypeStruct((m, n), jnp.bfloat16),
        scratch_shapes=[pltpu.VMEM((k, tn), jnp.bfloat16)],
        compiler_params=_params(("arbitrary", "arbitrary")),
        name=name,
    )(a, w_stack)


def _sigmoid(x):
    return 1.0 / (1.0 + jnp.exp(-x))


def _mixer_kernel(sinks_ref,
                  av_ref, ag_ref, bg_ref, cg_ref, bh_ref, q_ref, k_ref, v_ref,
                  hav_ref, hag_ref, hcg_ref, hbh_ref, hk_ref, hv_ref,
                  cw_ref, lng_ref, lnb_ref, sw_ref,
                  a_out, b_out, c_out,
                  glu_scr, conv_scr, cb_scr, kt_scr, vt_scr,
                  *, tm, tiles_per_seq):
    f32 = jnp.float32
    first = (pl.program_id(0) % tiles_per_seq) == 0
    keep = jnp.where(first, 0.0, 1.0).astype(f32)

    hal = BLOCK - CONF_HALO
    halo_glu = hav_ref[hal:, :].astype(f32) * _sigmoid(hag_ref[hal:, :].astype(f32)) * keep
    n_slab = CONF_WIDTH // 128
    for c in range(n_slab):
        glu_scr[c, 0:CONF_HALO, :] = halo_glu[:, c * 128:(c + 1) * 128]

    def glu_body(r, carry):
        rs = pl.ds(pl.multiple_of(r * 128, 128), 128)
        glu = av_ref[rs, :].astype(f32) * _sigmoid(ag_ref[rs, :].astype(f32))
        dst = pl.ds(pl.multiple_of(r * 128, 128) + CONF_HALO, 128)
        for c in range(n_slab):
            glu_scr[c, dst, :] = glu[:, c * 128:(c + 1) * 128]
        return carry
    lax.fori_loop(0, tm // 128, glu_body, 0)

    hbl = BLOCK - SCONV_HALO
    cb_scr[0:SCONV_HALO, :] = hcg_ref[hbl:, :].astype(f32) * hbh_ref[hbl:, :].astype(f32) * keep
    cb_scr[SCONV_HALO:, :] = cg_ref[...].astype(f32) * bh_ref[...].astype(f32)

    rows = 128
    half = rows // 2
    def conv_body(c, carry):
        lanes = pl.ds(pl.multiple_of(c * 128, 128), 128)
        for r in range(tm // rows):
            for par in range(2):
                acc = jnp.zeros((half, 128), f32)
                for j in range(CONF_KERNEL):
                    start = r * rows + par + CONF_HALO - (CONF_KERNEL - 1) + j
                    acc = acc + cw_ref[pl.ds(j, 1), lanes] * glu_scr[c, pl.ds(start, half, stride=2), :]
                conv_scr[c, pl.ds(r * rows + par, half, stride=2), :] = acc
            sc = jnp.zeros((rows, 128), f32)
            for j in range(SCONV_KERNEL):
                start = r * rows + SCONV_HALO - (SCONV_KERNEL - 1) + j
                sc = sc + sw_ref[pl.ds(j, 1), lanes] * cb_scr[pl.ds(start, rows), lanes]
            bg = bg_ref[pl.ds(r * rows, rows), lanes].astype(f32)
            b_out[pl.ds(r * rows, rows), lanes] = (bg * sc).astype(b_out.dtype)
        return carry
    lax.fori_loop(0, CONF_WIDTH // 128, conv_body, 0)

    def ln_body(r, carry):
        rs = pl.ds(pl.multiple_of(r * rows, rows), rows)
        y = jnp.concatenate([conv_scr[c, rs, :] for c in range(n_slab)], axis=-1)
        mu = jnp.mean(y, axis=-1, keepdims=True)
        yc = y - mu
        var = jnp.mean(yc * yc, axis=-1, keepdims=True)
        z = yc * lax.rsqrt(var + LN_EPS) * lng_ref[...] + lnb_ref[...]
        a_out[rs, :] = (z * _sigmoid(z)).astype(a_out.dtype)
        return carry
    lax.fori_loop(0, tm // rows, ln_body, 0)

    bf16 = jnp.bfloat16
    lane_lo = lax.broadcasted_iota(jnp.int32, (BLOCK, 128), 1) < HEAD_DIM

    def fill_tiled(dst_scr, src, row0):
        for pair in range(KV_WIDTH // 128):
            both = src[:, pair * 128:(pair + 1) * 128]
            swapped = pltpu.roll(both, HEAD_DIM, axis=1)
            even = jnp.where(lane_lo, both, swapped).astype(bf16)
            odd = jnp.where(lane_lo, swapped, both).astype(bf16)
            for rep in range(KV_WIDTH // 128):
                dst_scr[2 * pair, pl.ds(row0, BLOCK), rep * 128:(rep + 1) * 128] = even
                dst_scr[2 * pair + 1, pl.ds(row0, BLOCK), rep * 128:(rep + 1) * 128] = odd

    fill_tiled(kt_scr, hk_ref[...].astype(f32) * keep, 0)
    fill_tiled(vt_scr, hv_ref[...].astype(f32) * keep, 0)

    def fill_body(r, carry):
        rs = pl.ds(pl.multiple_of(r * BLOCK, BLOCK), BLOCK)
        row0 = pl.multiple_of(r * BLOCK, BLOCK) + BLOCK
        fill_tiled(kt_scr, k_ref[rs, :].astype(f32), row0)
        fill_tiled(vt_scr, v_ref[rs, :].astype(f32), row0)
        return carry
    lax.fori_loop(0, tm // BLOCK, fill_body, 0)

    qi = lax.broadcasted_iota(jnp.int32, (BLOCK, 2 * BLOCK), 0)
    kj = lax.broadcasted_iota(jnp.int32, (BLOCK, 2 * BLOCK), 1)
    band = (kj > qi) & (kj <= qi + WINDOW)
    lane_grp = kj // HEAD_DIM
    grp_mask = [jnp.where(lane_grp == g, 1.0, 0.0).astype(bf16) for g in range(GROUP)]
    contract_lanes = (((1,), (1,)), ((), ()))

    def attn_body(b, carry):
        qs = pl.ds(pl.multiple_of(b * BLOCK, BLOCK), BLOCK)
        ks = pl.ds(pl.multiple_of(b * BLOCK, BLOCK), 2 * BLOCK)
        has_prev = jnp.logical_or(jnp.logical_not(first), b > 0)
        valid = band & ((kj >= BLOCK) | has_prev)
        qblk = q_ref[qs, :] * (HEAD_DIM ** -0.5)
        for h in range(N_KV_HEADS):
            qh = qblk[:, h * GROUP * HEAD_DIM:(h + 1) * GROUP * HEAD_DIM]
            q_stack = jnp.concatenate([qh * grp_mask[g] for g in range(GROUP)], axis=0)
            s_all = lax.dot_general(q_stack, kt_scr[h, ks, :], contract_lanes,
                                    preferred_element_type=f32)
            probs = []
            for g in range(GROUP):
                s = jnp.where(valid, s_all[g * BLOCK:(g + 1) * BLOCK, :], -jnp.inf)
                sink = sinks_ref[h * GROUP + g]
                m = jnp.maximum(jnp.max(s, axis=-1, keepdims=True), sink)
                p = jnp.exp(s - m)
                denom = jnp.sum(p, axis=-1, keepdims=True) + jnp.exp(sink - m)
                probs.append((p * (1.0 / denom)).astype(bf16))
            o_all = jnp.dot(jnp.concatenate(probs, axis=0), vt_scr[h, ks, :],
                            preferred_element_type=f32)
            o = o_all[(GROUP - 1) * BLOCK:, :]
            for g in range(GROUP - 2, -1, -1):
                o = jnp.where(lane_grp == g, o_all[g * BLOCK:(g + 1) * BLOCK, :], o)
            c_out[qs, h * GROUP * HEAD_DIM:(h + 1) * GROUP * HEAD_DIM] = o.astype(c_out.dtype)
        return carry
    lax.fori_loop(0, tm // BLOCK, attn_body, 0)


def _mixer(u, sinks, conf_dw, ln_g, ln_b, sconv_w, *, seq, tm=512):
    t = u.shape[0]
    tiles_per_seq = seq // tm
    bpt = tm // BLOCK

    def col(width, off):
        return pl.BlockSpec((tm, width), lambda i: (i, off // width))

    def halo(width, off):
        return pl.BlockSpec((BLOCK, width), lambda i: (jnp.maximum(i * bpt - 1, 0), off // width))

    def whole(shape):
        return pl.BlockSpec(shape, lambda i: (0,) * len(shape))

    cw, kw = CONF_WIDTH, KV_WIDTH
    in_specs = [
        pl.BlockSpec(memory_space=pltpu.SMEM),
        col(cw, OFF_A_VAL), col(cw, OFF_A_GATE), col(cw, OFF_B_GATE), col(cw, OFF_C_GATE),
        col(cw, OFF_B_H), col(cw, OFF_Q), col(kw, OFF_K), col(kw, OFF_V),
        halo(cw, OFF_A_VAL), halo(cw, OFF_A_GATE), halo(cw, OFF_C_GATE), halo(cw, OFF_B_H),
        halo(kw, OFF_K), halo(kw, OFF_V),
        whole((CONF_KERNEL, cw)), whole((1, cw)), whole((1, cw)), whole((SCONV_KERNEL, cw)),
    ]
    out_spec = pl.BlockSpec((tm, cw), lambda i: (i, 0))
    out_sds = jax.ShapeDtypeStruct((t, cw), jnp.bfloat16)
    return pl.pallas_call(
        functools.partial(_mixer_kernel, tm=tm, tiles_per_seq=tiles_per_seq),
        grid=(t // tm,),
        in_specs=in_specs,
        out_specs=[out_spec, out_spec, out_spec],
        out_shape=[out_sds, out_sds, out_sds],
        scratch_shapes=[
            pltpu.VMEM((cw // 128, tm + CONF_HALO, 128), jnp.float32),
            pltpu.VMEM((cw // 128, tm, 128), jnp.float32),
            pltpu.VMEM((tm + SCONV_HALO, cw), jnp.float32),
            pltpu.VMEM((N_KV_HEADS, tm + BLOCK, GROUP * HEAD_DIM), jnp.bfloat16),
            pltpu.VMEM((N_KV_HEADS, tm + BLOCK, GROUP * HEAD_DIM), jnp.bfloat16),
        ],
        compiler_params=_params(("parallel",)),
        name="mixer",
    )(sinks, u, u, u, u, u, u, u, u, u, u, u, u, u, u,
      conf_dw, ln_g.reshape(1, cw), ln_b.reshape(1, cw), sconv_w)


def _gated_proj_kernel(a_ref, b_ref, c_ref, wa_ref, wb_ref, wc_ref,
                       g0_ref, g1_ref, g2_ref, gb_ref, o_ref):
    f32 = jnp.float32
    bf16 = jnp.bfloat16
    ya = jnp.dot(a_ref[...], wa_ref[...].astype(bf16), preferred_element_type=f32)
    yb = jnp.dot(b_ref[...], wb_ref[...].astype(bf16), preferred_element_type=f32)
    yc = jnp.dot(c_ref[...], wc_ref[...].astype(bf16), preferred_element_type=f32)
    g0 = _sigmoid(g0_ref[...].astype(f32) + gb_ref[0:1, :])
    g1 = _sigmoid(g1_ref[...].astype(f32) + gb_ref[1:2, :])
    g2 = _sigmoid(g2_ref[...].astype(f32) + gb_ref[2:3, :])
    o_ref[...] = (g0 * ya + g1 * yb + g2 * yc).astype(o_ref.dtype)


def _gated_proj(a, b, c, wa, wb, wc, layer, u, gate_b, *, tm=1024, tn=512):
    t = a.shape[0]
    kdim = a.shape[1]
    g_blk0 = OFF_G // tn
    per_gate = D_MODEL // tn
    act = pl.BlockSpec((tm, kdim), lambda i, j: (i, 0))
    wsp = pl.BlockSpec((None, kdim, tn), lambda i, j: (layer, 0, j))

    def gate(bi):
        return pl.BlockSpec((tm, tn), lambda i, j: (i, g_blk0 + bi * per_gate + j))

    return pl.pallas_call(
        _gated_proj_kernel,
        grid=(t // tm, D_MODEL // tn),
        in_specs=[act, act, act, wsp, wsp, wsp, gate(0), gate(1), gate(2),
                  pl.BlockSpec((N_BRANCH, tn), lambda i, j: (0, j))],
        out_specs=pl.BlockSpec((tm, tn), lambda i, j: (i, j)),
        out_shape=jax.ShapeDtypeStruct((t, D_MODEL), jnp.bfloat16),
        compiler_params=_params(("parallel", "parallel")),
        name="gated_proj",
    )(a, b, c, wa, wb, wc, u, u, u, gate_b)


def _mm_res_norm_kernel(a_ref, w_ref, x_ref, g_ref, xo_ref, ho_ref):
    k = pl.program_id(1)

    @pl.when(k == 0)
    def _():
        xo_ref[...] = x_ref[...]

    xo_ref[...] += jnp.dot(a_ref[...], w_ref[...], preferred_element_type=jnp.float32)

    @pl.when(k == pl.num_programs(1) - 1)
    def _():
        ho_ref[...] = _rmsnorm_rows(xo_ref[...], g_ref[...]).astype(ho_ref.dtype)


def _mm_res_norm(a, w_stack, layer, x, g, *, tm, tk, norm_dtype, name):
    m, kdim = a.shape
    n = w_stack.shape[2]
    return pl.pallas_call(
        _mm_res_norm_kernel,
        grid=(m // tm, kdim // tk),
        in_specs=[pl.BlockSpec((tm, tk), lambda i, k: (i, k)),
                  pl.BlockSpec((None, tk, n), lambda i, k: (layer, k, 0)),
                  pl.BlockSpec((tm, n), lambda i, k: (i, 0)),
                  pl.BlockSpec((1, n), lambda i, k: (0, 0))],
        out_specs=[pl.BlockSpec((tm, n), lambda i, k: (i, 0)),
                   pl.BlockSpec((tm, n), lambda i, k: (i, 0))],
        out_shape=[jax.ShapeDtypeStruct((m, n), jnp.float32),
                   jax.ShapeDtypeStruct((m, n), norm_dtype)],
        compiler_params=_params(("parallel", "arbitrary")),
        name=name,
    )(a, w_stack, x, g.reshape(1, n))


def kernel(x, norm_mix_g, w_in, gate_b, conf_dw, conf_ln_g, conf_ln_b, w_conf_out,
           sconv_w, w_sconv_out, sinks, w_attn_out, w_mix_out, norm_ffn_g,
           w_up, w_down, final_g):
    bsz, seq, d = x.shape
    depth = w_in.shape[0]
    t = bsz * seq
    bf16 = jnp.bfloat16
    xf = x.reshape(t, d)

    w_mix_bf = w_mix_out.astype(bf16)
    w_down_bf = w_down.astype(bf16)

    h = _rmsnorm(xf, norm_mix_g[0], bf16)
    for l in range(depth):
        u = _matmul_wcast(h, w_in, l, tm=1024, tn=1280, name="in_proj")
        a_act, b_act, c_act = _mixer(u, sinks[l], conf_dw[l], conf_ln_g[l], conf_ln_b[l],
                                     sconv_w[l], seq=seq)
        merged = _gated_proj(a_act, b_act, c_act, w_conf_out, w_sconv_out, w_attn_out, l,
                             u, gate_b[l])
        xf, h2 = _mm_res_norm(merged, w_mix_bf, l, xf, norm_ffn_g[l],
                              tm=512, tk=D_MODEL, norm_dtype=bf16, name="mix_out")
        hmid = _matmul_wcast(h2, w_up, l, tm=1024, tn=1024, relu2=True, name="mlp_up")
        last = l == depth - 1
        g_next = final_g if last else norm_mix_g[l + 1]
        xf, h = _mm_res_norm(hmid, w_down_bf, l, xf, g_next,
                             tm=512, tk=2048, norm_dtype=jnp.float32 if last else bf16,
                             name="mlp_down")
    return h.reshape(bsz, seq, d)
```

```python
import functools

import jax
import jax.numpy as jnp
from jax import lax
from jax.experimental import pallas as pl
from jax.experimental.pallas import tpu as pltpu

D_MODEL = 2048
CONF_WIDTH = 1024
CONF_KERNEL = 31
SCONV_WIDTH = 1024
SCONV_KERNEL = 3
HEAD_DIM = 64
N_Q_HEADS = 16
N_KV_HEADS = 4
GROUP = N_Q_HEADS // N_KV_HEADS
Q_WIDTH = N_Q_HEADS * HEAD_DIM
KV_WIDTH = N_KV_HEADS * HEAD_DIM
WINDOW = 128
BLOCK = 128
N_BRANCH = 3
D_FF = 4 * D_MODEL
RMS_EPS = 1e-6
LN_EPS = 1e-5

OFF_A_VAL = 0
OFF_A_GATE = OFF_A_VAL + CONF_WIDTH
OFF_B_GATE = OFF_A_GATE + CONF_WIDTH
OFF_C_GATE = OFF_B_GATE + SCONV_WIDTH
OFF_B_H = OFF_C_GATE + SCONV_WIDTH
OFF_Q = OFF_B_H + SCONV_WIDTH
OFF_K = OFF_Q + Q_WIDTH
OFF_V = OFF_K + KV_WIDTH
OFF_G = OFF_V + KV_WIDTH
IN_WIDTH = OFF_G + N_BRANCH * D_MODEL
QKV_WIDTH = Q_WIDTH + 2 * KV_WIDTH
assert GROUP * HEAD_DIM == 2 * BLOCK == KV_WIDTH

LANES = 128
CONF_HALO = 32
SCONV_HALO = 16
CAST_ROWS = 256
BRANCH_TM = 512
VMEM_LIMIT = 56 * 1024 * 1024


def _params(sem):
    return pltpu.CompilerParams(dimension_semantics=sem, vmem_limit_bytes=VMEM_LIMIT)


def _sigmoid(x):
    return 1.0 / (1.0 + jnp.exp(-x))


def _cast_weight(w_ref, wb_ref):
    def cast_rows(r, carry):
        rs = pl.ds(pl.multiple_of(r * CAST_ROWS, CAST_ROWS), CAST_ROWS)
        wb_ref[rs, :] = w_ref[rs, :].astype(wb_ref.dtype)
        return carry
    lax.fori_loop(0, w_ref.shape[0] // CAST_ROWS, cast_rows, 0)


def _rmsnorm_rows(x, g):
    ms = jnp.mean(x * x, axis=-1, keepdims=True)
    return x * lax.rsqrt(ms + RMS_EPS) * g


def _rmsnorm_kernel(x_ref, g_ref, o_ref):
    o_ref[...] = _rmsnorm_rows(x_ref[...], g_ref[...]).astype(o_ref.dtype)


def _rmsnorm(x, g, out_dtype, tm=512):
    t, d = x.shape
    return pl.pallas_call(
        _rmsnorm_kernel,
        grid=(t // tm,),
        in_specs=[pl.BlockSpec((tm, d), lambda i: (i, 0)),
                  pl.BlockSpec((1, d), lambda i: (0, 0))],
        out_specs=pl.BlockSpec((tm, d), lambda i: (i, 0)),
        out_shape=jax.ShapeDtypeStruct((t, d), out_dtype),
        compiler_params=_params(("parallel",)),
        name="rmsnorm",
    )(x, g.reshape(1, d))


def _mm_wcast_kernel(a_ref, w_ref, o_ref, wb_ref, *, relu2):
    @pl.when(pl.program_id(1) == 0)
    def _():
        _cast_weight(w_ref, wb_ref)

    acc = jnp.dot(a_ref[...], wb_ref[...], preferred_element_type=jnp.float32)
    if relu2:
        acc = jnp.square(jnp.maximum(acc, 0.0))
    o_ref[...] = acc.astype(o_ref.dtype)


def _matmul_wcast(a, w_stack, layer, *, tm, tn, relu2=False, name):
    m, k = a.shape
    n = w_stack.shape[2]
    return pl.pallas_call(
        functools.partial(_mm_wcast_kernel, relu2=relu2),
        grid=(n // tn, m // tm),
        in_specs=[pl.BlockSpec((tm, k), lambda j, i: (i, 0)),
                  pl.BlockSpec((None, k, tn), lambda j, i: (layer, 0, j))],
        out_specs=pl.BlockSpec((tm, tn), lambda j, i: (i, j)),
        out_shape=jax.ShapeDtypeStruct((m, n), jnp.bfloat16),
        scratch_shapes=[pltpu.VMEM((k, tn), jnp.bfloat16)],
        compiler_params=_params(("arbitrary", "arbitrary")),
        name=name,
    )(a, w_stack)


def _proj_cols_kernel(*refs, gate):
    if gate:
        a_ref, w_ref, b_ref, o_ref, wb_ref = refs
    else:
        a_ref, w_ref, o_ref, wb_ref = refs

    @pl.when(pl.program_id(1) == 0)
    def _():
        _cast_weight(w_ref, wb_ref)

    acc = jnp.dot(a_ref[...], wb_ref[...], preferred_element_type=jnp.float32)
    if gate:
        acc = _sigmoid(acc + b_ref[...])
    o_ref[...] = acc.astype(o_ref.dtype)


def _proj_cols(h, w_in, layer, col0, n, gate_bias=None, *, tm=1024, tn=1536, name):
    m, k = h.shape
    gate = gate_bias is not None
    in_specs = [pl.BlockSpec((tm, k), lambda j, i: (i, 0)),
                pl.BlockSpec((pl.Element(k), pl.Element(tn)),
                             lambda j, i: (layer * k, pl.multiple_of(j * tn + col0, LANES)))]
    operands = [h, w_in.reshape(-1, w_in.shape[2])]
    if gate:
        in_specs.append(pl.BlockSpec((1, tn), lambda j, i: (0, j)))
        operands.append(gate_bias.reshape(1, n))
    return pl.pallas_call(
        functools.partial(_proj_cols_kernel, gate=gate),
        grid=(n // tn, m // tm),
        in_specs=in_specs,
        out_specs=pl.BlockSpec((tm, tn), lambda j, i: (i, j)),
        out_shape=jax.ShapeDtypeStruct((m, n), jnp.bfloat16),
        scratch_shapes=[pltpu.VMEM((k, tn), jnp.bfloat16)],
        compiler_params=_params(("arbitrary", "arbitrary")),
        name=name,
    )(*operands)


def _conf_branch_kernel(h_ref, w_ref, cw_ref, lng_ref, lnb_ref, a_out,
                        wbf_scr, glu_a, glu_b, conv_scr, *, tm, tiles_per_seq):
    f32, bf16 = jnp.float32, jnp.bfloat16
    i = pl.program_id(0)
    n_slab = CONF_WIDTH // LANES

    @pl.when(i == 0)
    def _():
        def cast_rows(r, carry):
            rs = pl.ds(pl.multiple_of(r * CAST_ROWS, CAST_ROWS), CAST_ROWS)
            for c in range(n_slab):
                wbf_scr[rs, (2 * c) * LANES:(2 * c + 1) * LANES] = (
                    w_ref[rs, c * LANES:(c + 1) * LANES].astype(bf16))
                wbf_scr[rs, (2 * c + 1) * LANES:(2 * c + 2) * LANES] = (
                    w_ref[rs, CONF_WIDTH + c * LANES:CONF_WIDTH + (c + 1) * LANES].astype(bf16))
            return carry
        lax.fori_loop(0, w_ref.shape[0] // CAST_ROWS, cast_rows, 0)
        glu_a[...] = jnp.zeros_like(glu_a)
        glu_b[...] = jnp.zeros_like(glu_b)

    rows = 128
    half = rows // 2
    slabs_per_piece = 2
    piece = slabs_per_piece * 2 * LANES

    def step(cur, prv):
        keep = jnp.where(i % tiles_per_seq == 0, 0.0, 1.0).astype(f32)
        for c in range(n_slab):
            cur[c, 0:CONF_HALO, :] = prv[c, tm:tm + CONF_HALO, :] * keep

        def body(p, carry):
            cols = pl.ds(pl.multiple_of(p * piece, piece), piece)
            raw = jnp.dot(h_ref[...], wbf_scr[:, cols], preferred_element_type=f32)
            for s in range(slabs_per_piece):
                c = p * slabs_per_piece + s
                val = raw[:, (2 * s) * LANES:(2 * s + 1) * LANES]
                gate = raw[:, (2 * s + 1) * LANES:(2 * s + 2) * LANES]
                cur[c, pl.ds(CONF_HALO, tm), :] = val * _sigmoid(gate)
                lanes = pl.ds(pl.multiple_of(c * LANES, LANES), LANES)
                for r in range(tm // rows):
                    for par in range(2):
                        acc = jnp.zeros((half, LANES), f32)
                        for j in range(CONF_KERNEL):
                            start = r * rows + par + CONF_HALO - (CONF_KERNEL - 1) + j
                            acc = acc + cw_ref[pl.ds(j, 1), lanes] * prv[c, pl.ds(start, half, stride=2), :]
                        conv_scr[c, pl.ds(r * rows + par, half, stride=2), :] = acc
            return carry
        lax.fori_loop(0, n_slab // slabs_per_piece, body, 0)

    @pl.when(i % 2 == 0)
    def _():
        step(glu_a, glu_b)

    @pl.when(i % 2 == 1)
    def _():
        step(glu_b, glu_a)

    def ln_body(r, carry):
        rs = pl.ds(pl.multiple_of(r * rows, rows), rows)
        y = jnp.concatenate([conv_scr[c, rs, :] for c in range(n_slab)], axis=-1)
        mu = jnp.mean(y, axis=-1, keepdims=True)
        yc = y - mu
        var = jnp.mean(yc * yc, axis=-1, keepdims=True)
        z = yc * lax.rsqrt(var + LN_EPS) * lng_ref[...] + lnb_ref[...]
        a_out[rs, :] = (z * _sigmoid(z)).astype(a_out.dtype)
        return carry
    lax.fori_loop(0, tm // rows, ln_body, 0)


def _conf_branch(h, w_in, layer, conf_dw, ln_g, ln_b, *, seq, tm=BRANCH_TM):
    t, k = h.shape
    n_tiles = t // tm
    cw = CONF_WIDTH
    return pl.pallas_call(
        functools.partial(_conf_branch_kernel, tm=tm, tiles_per_seq=seq // tm),
        grid=(n_tiles + 1,),
        in_specs=[pl.BlockSpec((tm, k), lambda i: (jnp.minimum(i, n_tiles - 1), 0)),
                  pl.BlockSpec((None, k, 2 * cw), lambda i: (layer, 0, 0),
                               pipeline_mode=pl.Buffered(1)),
                  pl.BlockSpec((CONF_KERNEL, cw), lambda i: (0, 0)),
                  pl.BlockSpec((1, cw), lambda i: (0, 0)),
                  pl.BlockSpec((1, cw), lambda i: (0, 0))],
        out_specs=pl.BlockSpec((tm, cw), lambda i: (jnp.maximum(i - 1, 0), 0)),
        out_shape=jax.ShapeDtypeStruct((t, cw), jnp.bfloat16),
        scratch_shapes=[
            pltpu.VMEM((k, 2 * cw), jnp.bfloat16),
            pltpu.VMEM((cw // LANES, tm + CONF_HALO, LANES), jnp.float32),
            pltpu.VMEM((cw // LANES, tm + CONF_HALO, LANES), jnp.float32),
            pltpu.VMEM((cw // LANES, tm, LANES), jnp.float32),
        ],
        compiler_params=_params(("arbitrary",)),
        name="conf_branch",
    )(h, w_in, conf_dw, ln_g.reshape(1, cw), ln_b.reshape(1, cw))


def _attn_branch_kernel(sinks_ref, h_ref, w_ref, c_out,
                        wbf_scr, q_a, q_b, kt_a, kt_b, vt_a, vt_b, *, tm, tiles_per_seq):
    f32, bf16 = jnp.float32, jnp.bfloat16
    i = pl.program_id(0)
    n_blk = tm // BLOCK
    piece = QKV_WIDTH // (n_blk - 1)
    assert piece * (n_blk - 1) == QKV_WIDTH and Q_WIDTH % piece == 0 and piece == 2 * KV_WIDTH

    @pl.when(i == 0)
    def _():
        _cast_weight(w_ref, wbf_scr)
        for scr in (q_a, q_b, kt_a, kt_b, vt_a, vt_b):
            scr[...] = jnp.zeros_like(scr)

    lane_lo = lax.broadcasted_iota(jnp.int32, (BLOCK, LANES), 1) < HEAD_DIM
    qi = lax.broadcasted_iota(jnp.int32, (BLOCK, 2 * BLOCK), 0)
    kj = lax.broadcasted_iota(jnp.int32, (BLOCK, 2 * BLOCK), 1)
    band = (kj > qi) & (kj <= qi + WINDOW)
    lane_grp = kj // HEAD_DIM
    grp_mask = [jnp.where(lane_grp == g, 1.0, 0.0).astype(bf16) for g in range(GROUP)]
    contract_lanes = (((1,), (1,)), ((), ()))

    def fill_tiled(dst_scr, src, row0):
        for pair in range(KV_WIDTH // LANES):
            both = src[:, pair * LANES:(pair + 1) * LANES]
            swapped = pltpu.roll(both, HEAD_DIM, axis=1)
            even = jnp.where(lane_lo, both, swapped).astype(bf16)
            odd = jnp.where(lane_lo, swapped, both).astype(bf16)
            for rep in range(KV_WIDTH // LANES):
                dst_scr[2 * pair, pl.ds(row0, BLOCK), rep * LANES:(rep + 1) * LANES] = even
                dst_scr[2 * pair + 1, pl.ds(row0, BLOCK), rep * LANES:(rep + 1) * LANES] = odd

    def attend(b, q_scr, kt_scr, vt_scr, has_prev):
        valid = band & ((kj >= BLOCK) | has_prev)
        qblk = q_scr[b * BLOCK:(b + 1) * BLOCK, :]
        for h in range(N_KV_HEADS):
            qh = qblk[:, h * GROUP * HEAD_DIM:(h + 1) * GROUP * HEAD_DIM]
            q_stack = jnp.concatenate([qh * grp_mask[g] for g in range(GROUP)], axis=0)
            s_all = lax.dot_general(q_stack, kt_scr[h, b * BLOCK:(b + 2) * BLOCK, :], contract_lanes,
                                    preferred_element_type=f32)
            probs = []
            for g in range(GROUP):
                s = jnp.where(valid, s_all[g * BLOCK:(g + 1) * BLOCK, :], -jnp.inf)
                sink = sinks_ref[h * GROUP + g]
                m = jnp.maximum(jnp.max(s, axis=-1, keepdims=True), sink)
                p = jnp.exp(s - m)
                denom = jnp.sum(p, axis=-1, keepdims=True) + jnp.exp(sink - m)
                probs.append((p * (1.0 / denom)).astype(bf16))
            o_all = jnp.dot(jnp.concatenate(probs, axis=0), vt_scr[h, b * BLOCK:(b + 2) * BLOCK, :],
                            preferred_element_type=f32)
            o = o_all[(GROUP - 1) * BLOCK:, :]
            for g in range(GROUP - 2, -1, -1):
                o = jnp.where(lane_grp == g, o_all[g * BLOCK:(g + 1) * BLOCK, :], o)
            c_out[b * BLOCK:(b + 1) * BLOCK, h * GROUP * HEAD_DIM:(h + 1) * GROUP * HEAD_DIM] = (
                o.astype(c_out.dtype))

    def step(cur_q, cur_kt, cur_vt, prv_q, prv_kt, prv_vt):
        cur_first = i % tiles_per_seq == 0
        prv_first = (i + tiles_per_seq - 1) % tiles_per_seq == 0
        keep = jnp.where(cur_first, 0.0, 1.0).astype(f32)
        for h in range(N_KV_HEADS):
            cur_kt[h, 0:BLOCK, :] = (prv_kt[h, tm:tm + BLOCK, :].astype(f32) * keep).astype(bf16)
            cur_vt[h, 0:BLOCK, :] = (prv_vt[h, tm:tm + BLOCK, :].astype(f32) * keep).astype(bf16)
        for b in range(n_blk):
            if b < n_blk - 1:
                raw = jnp.dot(h_ref[...], wbf_scr[:, b * piece:(b + 1) * piece],
                              preferred_element_type=f32)
                if (b + 1) * piece <= Q_WIDTH:
                    cur_q[:, b * piece:(b + 1) * piece] = (raw * (HEAD_DIM ** -0.5)).astype(bf16)
                else:
                    for r in range(n_blk):
                        rr = slice(r * BLOCK, (r + 1) * BLOCK)
                        fill_tiled(cur_kt, raw[rr, 0:KV_WIDTH], (r + 1) * BLOCK)
                        fill_tiled(cur_vt, raw[rr, KV_WIDTH:2 * KV_WIDTH], (r + 1) * BLOCK)
            has_prev = jnp.logical_not(prv_first) if b == 0 else True
            attend(b, prv_q, prv_kt, prv_vt, has_prev)

    @pl.when(i % 2 == 0)
    def _():
        step(q_a, kt_a, vt_a, q_b, kt_b, vt_b)

    @pl.when(i % 2 == 1)
    def _():
        step(q_b, kt_b, vt_b, q_a, kt_a, vt_a)


def _attn_branch(h, w_in, layer, sinks, *, seq, tm=BRANCH_TM):
    t, k = h.shape
    n_tiles = t // tm
    kv_scr = pltpu.VMEM((N_KV_HEADS, tm + BLOCK, GROUP * HEAD_DIM), jnp.bfloat16)
    q_scr = pltpu.VMEM((tm, Q_WIDTH), jnp.bfloat16)
    return pl.pallas_call(
        functools.partial(_attn_branch_kernel, tm=tm, tiles_per_seq=seq // tm),
        grid=(n_tiles + 1,),
        in_specs=[pl.BlockSpec(memory_space=pltpu.SMEM),
                  pl.BlockSpec((tm, k), lambda i: (jnp.minimum(i, n_tiles - 1), 0)),
                  pl.BlockSpec((pl.Element(k), pl.Element(QKV_WIDTH)), lambda i: (layer * k, OFF_Q),
                               pipeline_mode=pl.Buffered(1))],
        out_specs=pl.BlockSpec((tm, Q_WIDTH), lambda i: (jnp.maximum(i - 1, 0), 0)),
        out_shape=jax.ShapeDtypeStruct((t, Q_WIDTH), jnp.bfloat16),
        scratch_shapes=[pltpu.VMEM((k, QKV_WIDTH), jnp.bfloat16),
                        q_scr, q_scr, kv_scr, kv_scr, kv_scr, kv_scr],
        compiler_params=_params(("arbitrary",)),
        name="attn_branch",
    )(sinks, h, w_in.reshape(-1, w_in.shape[2]))


def _gated_proj_kernel(a_ref, c_ref, bg_ref, cg_ref, bh_ref, hcg_ref, hbh_ref, sw_ref,
                       wa_ref, wb_ref, wc_ref, g0_ref, g1_ref, g2_ref, o_ref,
                       wa_bf, wb_bf, wc_bf, cb_scr, *, tm, tiles_per_seq):
    f32 = jnp.float32
    i = pl.program_id(1)

    @pl.when(i == 0)
    def _():
        _cast_weight(wa_ref, wa_bf)
        _cast_weight(wb_ref, wb_bf)
        _cast_weight(wc_ref, wc_bf)

    keep = jnp.where(i % tiles_per_seq == 0, 0.0, 1.0).astype(f32)
    cb_scr[0:SCONV_HALO, :] = hcg_ref[...].astype(f32) * hbh_ref[...].astype(f32) * keep
    cb_scr[SCONV_HALO:, :] = cg_ref[...].astype(f32) * bh_ref[...].astype(f32)
    sc = jnp.zeros((tm, SCONV_WIDTH), f32)
    for j in range(SCONV_KERNEL):
        start = SCONV_HALO - (SCONV_KERNEL - 1) + j
        sc = sc + sw_ref[pl.ds(j, 1), :] * cb_scr[pl.ds(start, tm), :]
    b_act = (bg_ref[...].astype(f32) * sc).astype(jnp.bfloat16)

    ya = jnp.dot(a_ref[...], wa_bf[...], preferred_element_type=f32)
    yb = jnp.dot(b_act, wb_bf[...], preferred_element_type=f32)
    yc = jnp.dot(c_ref[...], wc_bf[...], preferred_element_type=f32)
    merged = (g0_ref[...].astype(f32) * ya + g1_ref[...].astype(f32) * yb
              + g2_ref[...].astype(f32) * yc)
    o_ref[...] = merged.astype(o_ref.dtype)


def _gated_proj(a, c, u_s, gates, sconv_w, wa, wb, wc, layer, *, seq, tm=512, tn=1024):
    t, kdim = a.shape
    sw = SCONV_WIDTH
    per_gate = D_MODEL // tn
    halo_blk = tm // SCONV_HALO
    act = pl.BlockSpec((tm, kdim), lambda j, i: (i, 0))
    wsp = pl.BlockSpec((None, kdim, tn), lambda j, i: (layer, 0, j))
    wscr = pltpu.VMEM((kdim, tn), jnp.bfloat16)

    def ucol(cb):
        return pl.BlockSpec((tm, sw), lambda j, i: (i, cb))

    def uhalo(cb):
        return pl.BlockSpec((SCONV_HALO, sw), lambda j, i: (jnp.maximum(i * halo_blk - 1, 0), cb))

    def gate(bi):
        return pl.BlockSpec((tm, tn), lambda j, i: (i, bi * per_gate + j))

    return pl.pallas_call(
        functools.partial(_gated_proj_kernel, tm=tm, tiles_per_seq=seq // tm),
        grid=(D_MODEL // tn, t // tm),
        in_specs=[act, act, ucol(0), ucol(1), ucol(2), uhalo(1), uhalo(2),
                  pl.BlockSpec((SCONV_KERNEL, sw), lambda j, i: (0, 0)),
                  wsp, wsp, wsp, gate(0), gate(1), gate(2)],
        out_specs=pl.BlockSpec((tm, tn), lambda j, i: (i, j)),
        out_shape=jax.ShapeDtypeStruct((t, D_MODEL), jnp.bfloat16),
        scratch_shapes=[wscr, wscr, wscr, pltpu.VMEM((tm + SCONV_HALO, sw), jnp.float32)],
        compiler_params=_params(("arbitrary", "arbitrary")),
        name="gated_proj",
    )(a, c, u_s, u_s, u_s, u_s, u_s, sconv_w, wa, wb, wc, gates, gates, gates)


def _mm_res_norm_kernel(a_ref, w_ref, x_ref, g_ref, xo_ref, ho_ref):
    k = pl.program_id(1)

    @pl.when(k == 0)
    def _():
        xo_ref[...] = x_ref[...]

    xo_ref[...] += jnp.dot(a_ref[...], w_ref[...], preferred_element_type=jnp.float32)

    @pl.when(k == pl.num_programs(1) - 1)
    def _():
        ho_ref[...] = _rmsnorm_rows(xo_ref[...], g_ref[...]).astype(ho_ref.dtype)


def _mm_res_norm(a, w_stack, layer, x, g, *, tm, tk, norm_dtype, name):
    m, kdim = a.shape
    n = w_stack.shape[2]
    return pl.pallas_call(
        _mm_res_norm_kernel,
        grid=(m // tm, kdim // tk),
        in_specs=[pl.BlockSpec((tm, tk), lambda i, k: (i, k)),
                  pl.BlockSpec((None, tk, n), lambda i, k: (layer, k, 0)),
                  pl.BlockSpec((tm, n), lambda i, k: (i, 0)),
                  pl.BlockSpec((1, n), lambda i, k: (0, 0))],
        out_specs=[pl.BlockSpec((tm, n), lambda i, k: (i, 0)),
                   pl.BlockSpec((tm, n), lambda i, k: (i, 0))],
        out_shape=[jax.ShapeDtypeStruct((m, n), jnp.float32),
                   jax.ShapeDtypeStruct((m, n), norm_dtype)],
        compiler_params=_params(("parallel", "arbitrary")),
        name=name,
    )(a, w_stack, x, g.reshape(1, n))


def kernel(x, norm_mix_g, w_in, gate_b, conf_dw, conf_ln_g, conf_ln_b, w_conf_out,
           sconv_w, w_sconv_out, sinks, w_attn_out, w_mix_out, norm_ffn_g,
           w_up, w_down, final_g):
    bsz, seq, d = x.shape
    depth = w_in.shape[0]
    t = bsz * seq
    bf16 = jnp.bfloat16
    xf = x.reshape(t, d)

    w_mix_bf = w_mix_out.astype(bf16)
    w_down_bf = w_down.astype(bf16)

    h = _rmsnorm(xf, norm_mix_g[0], bf16)
    for l in range(depth):
        a_act = _conf_branch(h, w_in, l, conf_dw[l], conf_ln_g[l], conf_ln_b[l], seq=seq)
        c_act = _attn_branch(h, w_in, l, sinks[l], seq=seq)
        u_s = _proj_cols(h, w_in, l, OFF_B_GATE, 3 * SCONV_WIDTH, name="proj_sconv")
        gates = _proj_cols(h, w_in, l, OFF_G, N_BRANCH * D_MODEL, gate_b[l], name="proj_gates")
        merged = _gated_proj(a_act, c_act, u_s, gates, sconv_w[l], w_conf_out, w_sconv_out,
                             w_attn_out, l, seq=seq)
        xf, h2 = _mm_res_norm(merged, w_mix_bf, l, xf, norm_ffn_g[l],
                              tm=512, tk=D_MODEL, norm_dtype=bf16, name="mix_out")
        hmid = _matmul_wcast(h2, w_up, l, tm=1024, tn=1024, relu2=True, name="mlp_up")
        last = l == depth - 1
        g_next = final_g if last else norm_mix_g[l + 1]
        xf, h = _mm_res_norm(hmid, w_down_bf, l, xf, g_next,
                             tm=512, tk=2048, norm_dtype=jnp.float32 if last else bf16,
                             name="mlp_down")
    return h.reshape(bsz, seq, d)
```

```python
import functools

import jax
import jax.numpy as jnp
from jax import lax
from jax.experimental import pallas as pl
from jax.experimental.pallas import tpu as pltpu

D_MODEL = 2048
CONF_WIDTH = 1024
CONF_KERNEL = 31
SCONV_WIDTH = 1024
SCONV_KERNEL = 3
HEAD_DIM = 64
N_Q_HEADS = 16
N_KV_HEADS = 4
GROUP = N_Q_HEADS // N_KV_HEADS
Q_WIDTH = N_Q_HEADS * HEAD_DIM
KV_WIDTH = N_KV_HEADS * HEAD_DIM
WINDOW = 128
BLOCK = 128
N_BRANCH = 3
D_FF = 4 * D_MODEL
RMS_EPS = 1e-6
LN_EPS = 1e-5

OFF_A_VAL = 0
OFF_A_GATE = OFF_A_VAL + CONF_WIDTH
OFF_B_GATE = OFF_A_GATE + CONF_WIDTH
OFF_C_GATE = OFF_B_GATE + SCONV_WIDTH
OFF_B_H = OFF_C_GATE + SCONV_WIDTH
OFF_Q = OFF_B_H + SCONV_WIDTH
OFF_K = OFF_Q + Q_WIDTH
OFF_V = OFF_K + KV_WIDTH
OFF_G = OFF_V + KV_WIDTH
IN_WIDTH = OFF_G + N_BRANCH * D_MODEL
QKV_WIDTH = Q_WIDTH + 2 * KV_WIDTH
assert GROUP * HEAD_DIM == 2 * BLOCK == KV_WIDTH

LANES = 128
CONF_HALO = 32
SCONV_HALO = 16
CAST_ROWS = 256
BRANCH_TM = 512
VMEM_LIMIT = 56 * 1024 * 1024


def _params(sem):
    return pltpu.CompilerParams(dimension_semantics=sem, vmem_limit_bytes=VMEM_LIMIT)


def _sigmoid(x):
    return 0.5 * jnp.tanh(0.5 * x) + 0.5


def _cast_weight(w_ref, wb_ref):
    def cast_rows(r, carry):
        rs = pl.ds(pl.multiple_of(r * CAST_ROWS, CAST_ROWS), CAST_ROWS)
        wb_ref[rs, :] = w_ref[rs, :].astype(wb_ref.dtype)
        return carry
    lax.fori_loop(0, w_ref.shape[0] // CAST_ROWS, cast_rows, 0)


def _rmsnorm_rows(x, g):
    ms = jnp.mean(x * x, axis=-1, keepdims=True)
    return x * lax.rsqrt(ms + RMS_EPS) * g


def _rmsnorm_kernel(x_ref, g_ref, o_ref):
    o_ref[...] = _rmsnorm_rows(x_ref[...], g_ref[...]).astype(o_ref.dtype)


def _rmsnorm(x, g, out_dtype, tm=512):
    t, d = x.shape
    return pl.pallas_call(
        _rmsnorm_kernel,
        grid=(t // tm,),
        in_specs=[pl.BlockSpec((tm, d), lambda i: (i, 0)),
                  pl.BlockSpec((1, d), lambda i: (0, 0))],
        out_specs=pl.BlockSpec((tm, d), lambda i: (i, 0)),
        out_shape=jax.ShapeDtypeStruct((t, d), out_dtype),
        compiler_params=_params(("parallel",)),
        name="rmsnorm",
    )(x, g.reshape(1, d))


def _proj_kernel(*refs, gate, relu2, side):
    refs = list(refs)
    a_ref, w_ref = refs[:2]
    del refs[:2]
    b_ref = refs.pop(0) if gate else None
    side_ref = refs.pop(0) if side else None
    o_ref = refs.pop(0)
    side_out = refs.pop(0) if side else None
    (wb_ref,) = refs

    @pl.when(pl.program_id(1) == 0)
    def _():
        _cast_weight(w_ref, wb_ref)

    if side:
        side_out[...] = side_ref[...].astype(side_out.dtype)

    acc = jnp.dot(a_ref[...], wb_ref[...], preferred_element_type=jnp.float32)
    if gate:
        acc = _sigmoid(acc + b_ref[...])
    if relu2:
        acc = jnp.square(jnp.maximum(acc, 0.0))
    o_ref[...] = acc.astype(o_ref.dtype)


def _proj(a, w_stack, layer, col0, n, *, gate_bias=None, relu2=False, side=None, tm, tn, name):
    m, k = a.shape
    nj, ni = n // tn, m // tm
    in_specs = [pl.BlockSpec((tm, k), lambda j, i: (i, 0)),
                pl.BlockSpec((pl.Element(k), pl.Element(tn)),
                             lambda j, i: (layer * k, pl.multiple_of(j * tn + col0, LANES)))]
    operands = [a, w_stack.reshape(-1, w_stack.shape[2])]
    out_specs = [pl.BlockSpec((tm, tn), lambda j, i: (i, j))]
    out_shape = [jax.ShapeDtypeStruct((m, n), jnp.bfloat16)]
    if gate_bias is not None:
        in_specs.append(pl.BlockSpec((1, tn), lambda j, i: (0, j)))
        operands.append(gate_bias.reshape(1, n))
    if side is not None:
        _, s_rows, s_cols = side.shape
        rows_per = s_rows // (nj * ni)
        assert rows_per * nj * ni == s_rows and rows_per % 16 == 0
        blk0 = layer * (nj * ni)
        in_specs.append(pl.BlockSpec((rows_per, s_cols), lambda j, i: (blk0 + j * ni + i, 0)))
        operands.append(side.reshape(-1, s_cols))
        out_specs.append(pl.BlockSpec((rows_per, s_cols), lambda j, i: (j * ni + i, 0)))
        out_shape.append(jax.ShapeDtypeStruct((s_rows, s_cols), jnp.bfloat16))
    outs = pl.pallas_call(
        functools.partial(_proj_kernel, gate=gate_bias is not None, relu2=relu2, side=side is not None),
        grid=(nj, ni),
        in_specs=in_specs,
        out_specs=out_specs,
        out_shape=out_shape,
        scratch_shapes=[pltpu.VMEM((k, tn), jnp.bfloat16)],
        compiler_params=_params(("arbitrary", "arbitrary")),
        name=name,
    )(*operands)
    return outs if side is not None else outs[0]


def _conf_branch_kernel(h_ref, w_ref, cw_ref, lng_ref, lnb_ref, a_out,
                        wbf_scr, glu_scr, conv_scr, *, tm, tiles_per_seq):
    f32, bf16 = jnp.float32, jnp.bfloat16
    i = pl.program_id(0)
    n_slab = CONF_WIDTH // LANES

    @pl.when(i == 0)
    def _():
        def cast_rows(r, carry):
            rs = pl.ds(pl.multiple_of(r * CAST_ROWS, CAST_ROWS), CAST_ROWS)
            for c in range(n_slab):
                wbf_scr[rs, (2 * c) * LANES:(2 * c + 1) * LANES] = (
                    w_ref[rs, c * LANES:(c + 1) * LANES].astype(bf16))
                wbf_scr[rs, (2 * c + 1) * LANES:(2 * c + 2) * LANES] = (
                    w_ref[rs, CONF_WIDTH + c * LANES:CONF_WIDTH + (c + 1) * LANES].astype(bf16))
            return carry
        lax.fori_loop(0, w_ref.shape[0] // CAST_ROWS, cast_rows, 0)
        glu_scr[...] = jnp.zeros_like(glu_scr)

    rows = 128
    half = rows // 2

    keep = jnp.where(i % tiles_per_seq == 0, 0.0, 1.0).astype(f32)
    for c in range(n_slab):
        glu_scr[c, 0:CONF_HALO, :] = glu_scr[c, tm:tm + CONF_HALO, :] * keep

    raw = jnp.dot(h_ref[...], wbf_scr[...], preferred_element_type=f32)
    for c in range(n_slab):
        val = raw[:, (2 * c) * LANES:(2 * c + 1) * LANES]
        gate = raw[:, (2 * c + 1) * LANES:(2 * c + 2) * LANES]
        glu_scr[c, pl.ds(CONF_HALO, tm), :] = val * _sigmoid(gate)
        lanes = slice(c * LANES, (c + 1) * LANES)
        for r in range(tm // rows):
            for par in range(2):
                acc = jnp.zeros((half, LANES), f32)
                for j in range(CONF_KERNEL):
                    start = r * rows + par + CONF_HALO - (CONF_KERNEL - 1) + j
                    acc = acc + cw_ref[pl.ds(j, 1), lanes] * glu_scr[c, pl.ds(start, half, stride=2), :]
                conv_scr[c, pl.ds(r * rows + par, half, stride=2), :] = acc

    def ln_body(r, carry):
        rs = pl.ds(pl.multiple_of(r * rows, rows), rows)
        y = jnp.concatenate([conv_scr[c, rs, :] for c in range(n_slab)], axis=-1)
        mu = jnp.mean(y, axis=-1, keepdims=True)
        yc = y - mu
        var = jnp.mean(yc * yc, axis=-1, keepdims=True)
        z = yc * lax.rsqrt(var + LN_EPS) * lng_ref[...] + lnb_ref[...]
        a_out[rs, :] = (z * _sigmoid(z)).astype(a_out.dtype)
        return carry
    lax.fori_loop(0, tm // rows, ln_body, 0)


def _conf_branch(h, w_in, layer, conf_dw, ln_g, ln_b, *, seq, tm=BRANCH_TM):
    t, k = h.shape
    n_tiles = t // tm
    cw = CONF_WIDTH
    return pl.pallas_call(
        functools.partial(_conf_branch_kernel, tm=tm, tiles_per_seq=seq // tm),
        grid=(n_tiles,),
        in_specs=[pl.BlockSpec((tm, k), lambda i: (i, 0)),
                  pl.BlockSpec((None, k, 2 * cw), lambda i: (layer, 0, 0),
                               pipeline_mode=pl.Buffered(1)),
                  pl.BlockSpec((CONF_KERNEL, cw), lambda i: (0, 0)),
                  pl.BlockSpec((1, cw), lambda i: (0, 0)),
                  pl.BlockSpec((1, cw), lambda i: (0, 0))],
        out_specs=pl.BlockSpec((tm, cw), lambda i: (i, 0)),
        out_shape=jax.ShapeDtypeStruct((t, cw), jnp.bfloat16),
        scratch_shapes=[
            pltpu.VMEM((k, 2 * cw), jnp.bfloat16),
            pltpu.VMEM((cw // LANES, tm + CONF_HALO, LANES), jnp.float32),
            pltpu.VMEM((cw // LANES, tm, LANES), jnp.float32),
        ],
        compiler_params=_params(("arbitrary",)),
        name="conf_branch",
    )(h, w_in, conf_dw, ln_g.reshape(1, cw), ln_b.reshape(1, cw))


def _attn_branch_kernel(sinks_ref, h_ref, w_ref, c_out,
                        wbf_scr, q_a, q_b, kt_a, kt_b, vt_a, vt_b, *, tm, tiles_per_seq):
    f32, bf16 = jnp.float32, jnp.bfloat16
    i = pl.program_id(0)
    n_blk = tm // BLOCK
    piece = QKV_WIDTH // (n_blk - 1)
    assert piece * (n_blk - 1) == QKV_WIDTH and Q_WIDTH % piece == 0 and piece == 2 * KV_WIDTH

    @pl.when(i == 0)
    def _():
        _cast_weight(w_ref, wbf_scr)
        for scr in (q_a, q_b, kt_a, kt_b, vt_a, vt_b):
            scr[...] = jnp.zeros_like(scr)

    lane_lo = lax.broadcasted_iota(jnp.int32, (BLOCK, LANES), 1) < HEAD_DIM
    qi = lax.broadcasted_iota(jnp.int32, (BLOCK, 2 * BLOCK), 0)
    kj = lax.broadcasted_iota(jnp.int32, (BLOCK, 2 * BLOCK), 1)
    band = (kj > qi) & (kj <= qi + WINDOW)
    lane_grp = kj // HEAD_DIM
    grp_mask = [jnp.where(lane_grp == g, 1.0, 0.0).astype(bf16) for g in range(GROUP)]
    contract_lanes = (((1,), (1,)), ((), ()))

    def fill_tiled(dst_scr, src, row0):
        for pair in range(KV_WIDTH // LANES):
            both = src[:, pair * LANES:(pair + 1) * LANES]
            swapped = pltpu.roll(both, HEAD_DIM, axis=1)
            even = jnp.where(lane_lo, both, swapped).astype(bf16)
            odd = jnp.where(lane_lo, swapped, both).astype(bf16)
            for rep in range(KV_WIDTH // LANES):
                dst_scr[2 * pair, pl.ds(row0, BLOCK), rep * LANES:(rep + 1) * LANES] = even
                dst_scr[2 * pair + 1, pl.ds(row0, BLOCK), rep * LANES:(rep + 1) * LANES] = odd

    def attend(b, q_scr, kt_scr, vt_scr, has_prev):
        valid = band & ((kj >= BLOCK) | has_prev)
        qblk = q_scr[b * BLOCK:(b + 1) * BLOCK, :]
        for h in range(N_KV_HEADS):
            qh = qblk[:, h * GROUP * HEAD_DIM:(h + 1) * GROUP * HEAD_DIM]
            q_stack = jnp.concatenate([qh * grp_mask[g] for g in range(GROUP)], axis=0)
            s_all = lax.dot_general(q_stack, kt_scr[h, b * BLOCK:(b + 2) * BLOCK, :], contract_lanes,
                                    preferred_element_type=f32)
            probs = []
            for g in range(GROUP):
                s = jnp.where(valid, s_all[g * BLOCK:(g + 1) * BLOCK, :], -jnp.inf)
                sink = sinks_ref[h * GROUP + g]
                m = jnp.maximum(jnp.max(s, axis=-1, keepdims=True), sink)
                p = jnp.exp(s - m)
                denom = jnp.sum(p, axis=-1, keepdims=True) + jnp.exp(sink - m)
                probs.append((p * (1.0 / denom)).astype(bf16))
            o_all = jnp.dot(jnp.concatenate(probs, axis=0), vt_scr[h, b * BLOCK:(b + 2) * BLOCK, :],
                            preferred_element_type=f32)
            o = o_all[(GROUP - 1) * BLOCK:, :]
            for g in range(GROUP - 2, -1, -1):
                o = jnp.where(lane_grp == g, o_all[g * BLOCK:(g + 1) * BLOCK, :], o)
            c_out[b * BLOCK:(b + 1) * BLOCK, h * GROUP * HEAD_DIM:(h + 1) * GROUP * HEAD_DIM] = (
                o.astype(c_out.dtype))

    def step(cur_q, cur_kt, cur_vt, prv_q, prv_kt, prv_vt):
        cur_first = i % tiles_per_seq == 0
        prv_first = (i + tiles_per_seq - 1) % tiles_per_seq == 0
        keep = jnp.where(cur_first, 0.0, 1.0).astype(f32)
        for h in range(N_KV_HEADS):
            cur_kt[h, 0:BLOCK, :] = (prv_kt[h, tm:tm + BLOCK, :].astype(f32) * keep).astype(bf16)
            cur_vt[h, 0:BLOCK, :] = (prv_vt[h, tm:tm + BLOCK, :].astype(f32) * keep).astype(bf16)
        for b in range(n_blk):
            if b < n_blk - 1:
                raw = jnp.dot(h_ref[...], wbf_scr[:, b * piece:(b + 1) * piece],
                              preferred_element_type=f32)
                if (b + 1) * piece <= Q_WIDTH:
                    cur_q[:, b * piece:(b + 1) * piece] = (raw * (HEAD_DIM ** -0.5)).astype(bf16)
                else:
                    for r in range(n_blk):
                        rr = slice(r * BLOCK, (r + 1) * BLOCK)
                        fill_tiled(cur_kt, raw[rr, 0:KV_WIDTH], (r + 1) * BLOCK)
                        fill_tiled(cur_vt, raw[rr, KV_WIDTH:2 * KV_WIDTH], (r + 1) * BLOCK)
            has_prev = jnp.logical_not(prv_first) if b == 0 else True
            attend(b, prv_q, prv_kt, prv_vt, has_prev)

    @pl.when(i % 2 == 0)
    def _():
        step(q_a, kt_a, vt_a, q_b, kt_b, vt_b)

    @pl.when(i % 2 == 1)
    def _():
        step(q_b, kt_b, vt_b, q_a, kt_a, vt_a)


def _attn_branch(h, w_in, layer, sinks, *, seq, tm=BRANCH_TM):
    t, k = h.shape
    n_tiles = t // tm
    kv_scr = pltpu.VMEM((N_KV_HEADS, tm + BLOCK, GROUP * HEAD_DIM), jnp.bfloat16)
    q_scr = pltpu.VMEM((tm, Q_WIDTH), jnp.bfloat16)
    return pl.pallas_call(
        functools.partial(_attn_branch_kernel, tm=tm, tiles_per_seq=seq // tm),
        grid=(n_tiles + 1,),
        in_specs=[pl.BlockSpec(memory_space=pltpu.SMEM),
                  pl.BlockSpec((tm, k), lambda i: (jnp.minimum(i, n_tiles - 1), 0)),
                  pl.BlockSpec((pl.Element(k), pl.Element(QKV_WIDTH)), lambda i: (layer * k, OFF_Q),
                               pipeline_mode=pl.Buffered(1))],
        out_specs=pl.BlockSpec((tm, Q_WIDTH), lambda i: (jnp.maximum(i - 1, 0), 0)),
        out_shape=jax.ShapeDtypeStruct((t, Q_WIDTH), jnp.bfloat16),
        scratch_shapes=[pltpu.VMEM((k, QKV_WIDTH), jnp.bfloat16),
                        q_scr, q_scr, kv_scr, kv_scr, kv_scr, kv_scr],
        compiler_params=_params(("arbitrary",)),
        name="attn_branch",
    )(sinks, h, w_in.reshape(-1, w_in.shape[2]))


def _gated_proj_kernel(a_ref, c_ref, bg_ref, cg_ref, bh_ref, hcg_ref, hbh_ref, sw_ref,
                       wa_ref, wb_ref, wc_ref, g0_ref, g1_ref, g2_ref, o_ref,
                       wa_bf, wb_bf, wc_bf, cb_scr, *, tm, tiles_per_seq):
    f32 = jnp.float32
    i = pl.program_id(1)

    @pl.when(i == 0)
    def _():
        _cast_weight(wa_ref, wa_bf)
        _cast_weight(wb_ref, wb_bf)
        _cast_weight(wc_ref, wc_bf)

    keep = jnp.where(i % tiles_per_seq == 0, 0.0, 1.0).astype(f32)
    cb_scr[0:SCONV_HALO, :] = hcg_ref[...].astype(f32) * hbh_ref[...].astype(f32) * keep
    cb_scr[SCONV_HALO:, :] = cg_ref[...].astype(f32) * bh_ref[...].astype(f32)
    sc = jnp.zeros((tm, SCONV_WIDTH), f32)
    for j in range(SCONV_KERNEL):
        start = SCONV_HALO - (SCONV_KERNEL - 1) + j
        sc = sc + sw_ref[pl.ds(j, 1), :] * cb_scr[pl.ds(start, tm), :]
    b_act = (bg_ref[...].astype(f32) * sc).astype(jnp.bfloat16)

    ya = jnp.dot(a_ref[...], wa_bf[...], preferred_element_type=f32)
    yb = jnp.dot(b_act, wb_bf[...], preferred_element_type=f32)
    yc = jnp.dot(c_ref[...], wc_bf[...], preferred_element_type=f32)
    merged = (g0_ref[...].astype(f32) * ya + g1_ref[...].astype(f32) * yb
              + g2_ref[...].astype(f32) * yc)
    o_ref[...] = merged.astype(o_ref.dtype)


def _gated_proj(a, c, u_s, gates, sconv_w, wa, wb, wc, layer, *, seq, tm=512, tn=1024):
    t, kdim = a.shape
    sw = SCONV_WIDTH
    per_gate = D_MODEL // tn
    halo_blk = tm // SCONV_HALO
    act = pl.BlockSpec((tm, kdim), lambda j, i: (i, 0))
    wsp = pl.BlockSpec((None, kdim, tn), lambda j, i: (layer, 0, j))
    wscr = pltpu.VMEM((kdim, tn), jnp.bfloat16)

    def ucol(cb):
        return pl.BlockSpec((tm, sw), lambda j, i: (i, cb))

    def uhalo(cb):
        return pl.BlockSpec((SCONV_HALO, sw), lambda j, i: (jnp.maximum(i * halo_blk - 1, 0), cb))

    def gate(bi):
        return pl.BlockSpec((tm, tn), lambda j, i: (i, bi * per_gate + j))

    return pl.pallas_call(
        functools.partial(_gated_proj_kernel, tm=tm, tiles_per_seq=seq // tm),
        grid=(D_MODEL // tn, t // tm),
        in_specs=[act, act, ucol(0), ucol(1), ucol(2), uhalo(1), uhalo(2),
                  pl.BlockSpec((SCONV_KERNEL, sw), lambda j, i: (0, 0)),
                  wsp, wsp, wsp, gate(0), gate(1), gate(2)],
        out_specs=pl.BlockSpec((tm, tn), lambda j, i: (i, j)),
        out_shape=jax.ShapeDtypeStruct((t, D_MODEL), jnp.bfloat16),
        scratch_shapes=[wscr, wscr, wscr, pltpu.VMEM((tm + SCONV_HALO, sw), jnp.float32)],
        compiler_params=_params(("arbitrary", "arbitrary")),
        name="gated_proj",
    )(a, c, u_s, u_s, u_s, u_s, u_s, sconv_w, wa, wb, wc, gates, gates, gates)


def _mm_res_norm_kernel(a_ref, w_ref, x_ref, g_ref, xo_ref, ho_ref):
    k = pl.program_id(1)

    @pl.when(k == 0)
    def _():
        xo_ref[...] = x_ref[...]

    xo_ref[...] += jnp.dot(a_ref[...], w_ref[...], preferred_element_type=jnp.float32)

    @pl.when(k == pl.num_programs(1) - 1)
    def _():
        ho_ref[...] = _rmsnorm_rows(xo_ref[...], g_ref[...]).astype(ho_ref.dtype)


def _mm_res_norm(a, w, x, g, *, tm, tk, norm_dtype, name):
    m, kdim = a.shape
    n = w.shape[1]
    return pl.pallas_call(
        _mm_res_norm_kernel,
        grid=(m // tm, kdim // tk),
        in_specs=[pl.BlockSpec((tm, tk), lambda i, k: (i, k)),
                  pl.BlockSpec((tk, n), lambda i, k: (k, 0)),
                  pl.BlockSpec((tm, n), lambda i, k: (i, 0)),
                  pl.BlockSpec((1, n), lambda i, k: (0, 0))],
        out_specs=[pl.BlockSpec((tm, n), lambda i, k: (i, 0)),
                   pl.BlockSpec((tm, n), lambda i, k: (i, 0))],
        out_shape=[jax.ShapeDtypeStruct((m, n), jnp.float32),
                   jax.ShapeDtypeStruct((m, n), norm_dtype)],
        compiler_params=_params(("parallel", "arbitrary")),
        name=name,
    )(a, w, x, g.reshape(1, n))


def kernel(x, norm_mix_g, w_in, gate_b, conf_dw, conf_ln_g, conf_ln_b, w_conf_out,
           sconv_w, w_sconv_out, sinks, w_attn_out, w_mix_out, norm_ffn_g,
           w_up, w_down, final_g):
    bsz, seq, d = x.shape
    depth = w_in.shape[0]
    t = bsz * seq
    bf16 = jnp.bfloat16
    xf = x.reshape(t, d)

    h = _rmsnorm(xf, norm_mix_g[0], bf16)
    for l in range(depth):
        a_act = _conf_branch(h, w_in, l, conf_dw[l], conf_ln_g[l], conf_ln_b[l], seq=seq)
        c_act = _attn_branch(h, w_in, l, sinks[l], seq=seq)
        u_s, w_mix_bf = _proj(h, w_in, l, OFF_B_GATE, 3 * SCONV_WIDTH, side=w_mix_out,
                              tm=1024, tn=1536, name="proj_sconv")
        gates = _proj(h, w_in, l, OFF_G, N_BRANCH * D_MODEL, gate_bias=gate_b[l],
                      tm=1024, tn=1536, name="proj_gates")
        merged = _gated_proj(a_act, c_act, u_s, gates, sconv_w[l], w_conf_out, w_sconv_out,
                             w_attn_out, l, seq=seq)
        xf, h2 = _mm_res_norm(merged, w_mix_bf, xf, norm_ffn_g[l],
                              tm=512, tk=D_MODEL, norm_dtype=bf16, name="mix_out")
        hmid, w_down_bf = _proj(h2, w_up, l, 0, D_FF, relu2=True, side=w_down,
                                tm=1024, tn=1024, name="mlp_up")
        last = l == depth - 1
        g_next = final_g if last else norm_mix_g[l + 1]
        xf, h = _mm_res_norm(hmid, w_down_bf, xf, g_next,
                             tm=512, tk=2048, norm_dtype=jnp.float32 if last else bf16,
                             name="mlp_down")
    return h.reshape(bsz, seq, d)
```

```python
import functools

import jax
import jax.numpy as jnp
from jax import lax
from jax.experimental import pallas as pl
from jax.experimental.pallas import tpu as pltpu

D_MODEL = 2048
CONF_WIDTH = 1024
CONF_KERNEL = 31
SCONV_WIDTH = 1024
SCONV_KERNEL = 3
HEAD_DIM = 64
N_Q_HEADS = 16
N_KV_HEADS = 4
GROUP = N_Q_HEADS // N_KV_HEADS
Q_WIDTH = N_Q_HEADS * HEAD_DIM
KV_WIDTH = N_KV_HEADS * HEAD_DIM
WINDOW = 128
BLOCK = 128
N_BRANCH = 3
D_FF = 4 * D_MODEL
RMS_EPS = 1e-6
LN_EPS = 1e-5

OFF_A_VAL = 0
OFF_A_GATE = OFF_A_VAL + CONF_WIDTH
OFF_B_GATE = OFF_A_GATE + CONF_WIDTH
OFF_C_GATE = OFF_B_GATE + SCONV_WIDTH
OFF_B_H = OFF_C_GATE + SCONV_WIDTH
OFF_Q = OFF_B_H + SCONV_WIDTH
OFF_K = OFF_Q + Q_WIDTH
OFF_V = OFF_K + KV_WIDTH
OFF_G = OFF_V + KV_WIDTH
IN_WIDTH = OFF_G + N_BRANCH * D_MODEL
QKV_WIDTH = Q_WIDTH + 2 * KV_WIDTH
assert GROUP * HEAD_DIM == 2 * BLOCK == KV_WIDTH

LANES = 128
CONF_HALO = 32
SCONV_HALO = 16
CAST_ROWS = 256
BRANCH_TM = 512
VMEM_LIMIT = 56 * 1024 * 1024


def _params(sem):
    return pltpu.CompilerParams(dimension_semantics=sem, vmem_limit_bytes=VMEM_LIMIT)


def _sigmoid(x):
    return 0.5 * jnp.tanh(0.5 * x) + 0.5


def _cast_weight(w_ref, wb_ref):
    def cast_rows(r, carry):
        rs = pl.ds(pl.multiple_of(r * CAST_ROWS, CAST_ROWS), CAST_ROWS)
        wb_ref[rs, :] = w_ref[rs, :].astype(wb_ref.dtype)
        return carry
    lax.fori_loop(0, w_ref.shape[0] // CAST_ROWS, cast_rows, 0)


def _rmsnorm_rows(x, g):
    ms = jnp.mean(x * x, axis=-1, keepdims=True)
    return x * lax.rsqrt(ms + RMS_EPS) * g


def _rmsnorm_kernel(x_ref, g_ref, o_ref):
    o_ref[...] = _rmsnorm_rows(x_ref[...], g_ref[...]).astype(o_ref.dtype)


def _rmsnorm(x, g, out_dtype, tm=512):
    t, d = x.shape
    return pl.pallas_call(
        _rmsnorm_kernel,
        grid=(t // tm,),
        in_specs=[pl.BlockSpec((tm, d), lambda i: (i, 0)),
                  pl.BlockSpec((1, d), lambda i: (0, 0))],
        out_specs=pl.BlockSpec((tm, d), lambda i: (i, 0)),
        out_shape=jax.ShapeDtypeStruct((t, d), out_dtype),
        compiler_params=_params(("parallel",)),
        name="rmsnorm",
    )(x, g.reshape(1, d))


def _proj_kernel(*refs, gate, relu2, side):
    refs = list(refs)
    a_ref, w_ref = refs[:2]
    del refs[:2]
    b_ref = refs.pop(0) if gate else None
    side_ref = refs.pop(0) if side else None
    o_ref = refs.pop(0)
    side_out = refs.pop(0) if side else None
    (wb_ref,) = refs

    @pl.when(pl.program_id(1) == 0)
    def _():
        _cast_weight(w_ref, wb_ref)

    if side:
        side_out[...] = side_ref[...].astype(side_out.dtype)

    acc = jnp.dot(a_ref[...], wb_ref[...], preferred_element_type=jnp.float32)
    if gate:
        acc = _sigmoid(acc + b_ref[...])
    if relu2:
        acc = jnp.square(jnp.maximum(acc, 0.0))
    o_ref[...] = acc.astype(o_ref.dtype)


def _proj(a, w_stack, layer, col0, n, *, gate_bias=None, relu2=False, side=None, tm, tn, name):
    m, k = a.shape
    nj, ni = n // tn, m // tm
    in_specs = [pl.BlockSpec((tm, k), lambda j, i: (i, 0)),
                pl.BlockSpec((pl.Element(k), pl.Element(tn)),
                             lambda j, i: (layer * k, pl.multiple_of(j * tn + col0, LANES)))]
    operands = [a, w_stack.reshape(-1, w_stack.shape[2])]
    out_specs = [pl.BlockSpec((tm, tn), lambda j, i: (i, j))]
    out_shape = [jax.ShapeDtypeStruct((m, n), jnp.bfloat16)]
    if gate_bias is not None:
        in_specs.append(pl.BlockSpec((1, tn), lambda j, i: (0, j)))
        operands.append(gate_bias.reshape(1, n))
    if side is not None:
        _, s_rows, s_cols = side.shape
        rows_per = s_rows // (nj * ni)
        assert rows_per * nj * ni == s_rows and rows_per % 16 == 0
        blk0 = layer * (nj * ni)
        in_specs.append(pl.BlockSpec((rows_per, s_cols), lambda j, i: (blk0 + j * ni + i, 0)))
        operands.append(side.reshape(-1, s_cols))
        out_specs.append(pl.BlockSpec((rows_per, s_cols), lambda j, i: (j * ni + i, 0)))
        out_shape.append(jax.ShapeDtypeStruct((s_rows, s_cols), jnp.bfloat16))
    outs = pl.pallas_call(
        functools.partial(_proj_kernel, gate=gate_bias is not None, relu2=relu2, side=side is not None),
        grid=(nj, ni),
        in_specs=in_specs,
        out_specs=out_specs,
        out_shape=out_shape,
        scratch_shapes=[pltpu.VMEM((k, tn), jnp.bfloat16)],
        compiler_params=_params(("arbitrary", "arbitrary")),
        name=name,
    )(*operands)
    return outs if side is not None else outs[0]


def _conf_branch_kernel(h_ref, w_ref, cw_ref, lng_ref, lnb_ref, a_out,
                        wbf_scr, glu_scr, conv_scr, *, tm, tiles_per_seq):
    f32, bf16 = jnp.float32, jnp.bfloat16
    i = pl.program_id(0)
    n_slab = CONF_WIDTH // LANES

    @pl.when(i == 0)
    def _():
        def cast_rows(r, carry):
            rs = pl.ds(pl.multiple_of(r * CAST_ROWS, CAST_ROWS), CAST_ROWS)
            for c in range(n_slab):
                wbf_scr[rs, (2 * c) * LANES:(2 * c + 1) * LANES] = (
                    w_ref[rs, c * LANES:(c + 1) * LANES].astype(bf16))
                wbf_scr[rs, (2 * c + 1) * LANES:(2 * c + 2) * LANES] = (
                    w_ref[rs, CONF_WIDTH + c * LANES:CONF_WIDTH + (c + 1) * LANES].astype(bf16))
            return carry
        lax.fori_loop(0, w_ref.shape[0] // CAST_ROWS, cast_rows, 0)
        glu_scr[...] = jnp.zeros_like(glu_scr)

    rows = 128
    half = rows // 2

    keep = jnp.where(i % tiles_per_seq == 0, 0.0, 1.0).astype(f32)
    for c in range(n_slab):
        glu_scr[c, 0:CONF_HALO, :] = glu_scr[c, tm:tm + CONF_HALO, :] * keep

    raw = jnp.dot(h_ref[...], wbf_scr[...], preferred_element_type=f32)
    for c in range(n_slab):
        val = raw[:, (2 * c) * LANES:(2 * c + 1) * LANES]
        gate = raw[:, (2 * c + 1) * LANES:(2 * c + 2) * LANES]
        glu_scr[c, pl.ds(CONF_HALO, tm), :] = val * _sigmoid(gate)
        lanes = slice(c * LANES, (c + 1) * LANES)
        for r in range(tm // rows):
            for par in range(2):
                acc = jnp.zeros((half, LANES), f32)
                for j in range(CONF_KERNEL):
                    start = r * rows + par + CONF_HALO - (CONF_KERNEL - 1) + j
                    acc = acc + cw_ref[pl.ds(j, 1), lanes] * glu_scr[c, pl.ds(start, half, stride=2), :]
                conv_scr[c, pl.ds(r * rows + par, half, stride=2), :] = acc

    def ln_body(r, carry):
        rs = pl.ds(pl.multiple_of(r * rows, rows), rows)
        y = jnp.concatenate([conv_scr[c, rs, :] for c in range(n_slab)], axis=-1)
        mu = jnp.mean(y, axis=-1, keepdims=True)
        yc = y - mu
        var = jnp.mean(yc * yc, axis=-1, keepdims=True)
        z = yc * lax.rsqrt(var + LN_EPS) * lng_ref[...] + lnb_ref[...]
        a_out[rs, :] = (z * _sigmoid(z)).astype(a_out.dtype)
        return carry
    lax.fori_loop(0, tm // rows, ln_body, 0)


def _conf_branch(h, w_in, layer, conf_dw, ln_g, ln_b, *, seq, tm=BRANCH_TM):
    t, k = h.shape
    n_tiles = t // tm
    cw = CONF_WIDTH
    return pl.pallas_call(
        functools.partial(_conf_branch_kernel, tm=tm, tiles_per_seq=seq // tm),
        grid=(n_tiles,),
        in_specs=[pl.BlockSpec((tm, k), lambda i: (i, 0)),
                  pl.BlockSpec((None, k, 2 * cw), lambda i: (layer, 0, 0),
                               pipeline_mode=pl.Buffered(1)),
                  pl.BlockSpec((CONF_KERNEL, cw), lambda i: (0, 0)),
                  pl.BlockSpec((1, cw), lambda i: (0, 0)),
                  pl.BlockSpec((1, cw), lambda i: (0, 0))],
        out_specs=pl.BlockSpec((tm, cw), lambda i: (i, 0)),
        out_shape=jax.ShapeDtypeStruct((t, cw), jnp.bfloat16),
        scratch_shapes=[
            pltpu.VMEM((k, 2 * cw), jnp.bfloat16),
            pltpu.VMEM((cw // LANES, tm + CONF_HALO, LANES), jnp.float32),
            pltpu.VMEM((cw // LANES, tm, LANES), jnp.float32),
        ],
        compiler_params=_params(("arbitrary",)),
        name="conf_branch",
    )(h, w_in, conf_dw, ln_g.reshape(1, cw), ln_b.reshape(1, cw))


def _attn_branch_kernel(sinks_ref, h_ref, w_ref, c_out,
                        wbf_scr, q_scr, kt_scr, vt_scr, *, tm, tiles_per_seq):
    f32, bf16 = jnp.float32, jnp.bfloat16
    i = pl.program_id(0)
    n_blk = tm // BLOCK

    @pl.when(i == 0)
    def _():
        _cast_weight(w_ref, wbf_scr)
        kt_scr[...] = jnp.zeros_like(kt_scr)
        vt_scr[...] = jnp.zeros_like(vt_scr)

    lane_lo = lax.broadcasted_iota(jnp.int32, (BLOCK, LANES), 1) < HEAD_DIM
    qi = lax.broadcasted_iota(jnp.int32, (BLOCK, 2 * BLOCK), 0)
    kj = lax.broadcasted_iota(jnp.int32, (BLOCK, 2 * BLOCK), 1)
    band = (kj > qi) & (kj <= qi + WINDOW)
    lane_grp = kj // HEAD_DIM
    grp_mask = [jnp.where(lane_grp == g, 1.0, 0.0).astype(bf16) for g in range(GROUP)]
    contract_lanes = (((1,), (1,)), ((), ()))

    def fill_tiled(dst_scr, src, row0):
        for pair in range(KV_WIDTH // LANES):
            both = src[:, pair * LANES:(pair + 1) * LANES]
            swapped = pltpu.roll(both, HEAD_DIM, axis=1)
            even = jnp.where(lane_lo, both, swapped).astype(bf16)
            odd = jnp.where(lane_lo, swapped, both).astype(bf16)
            for rep in range(KV_WIDTH // LANES):
                dst_scr[2 * pair, pl.ds(row0, BLOCK), rep * LANES:(rep + 1) * LANES] = even
                dst_scr[2 * pair + 1, pl.ds(row0, BLOCK), rep * LANES:(rep + 1) * LANES] = odd

    def attend(b, has_prev):
        valid = band & ((kj >= BLOCK) | has_prev)
        qblk = q_scr[b * BLOCK:(b + 1) * BLOCK, :]
        for h in range(N_KV_HEADS):
            qh = qblk[:, h * GROUP * HEAD_DIM:(h + 1) * GROUP * HEAD_DIM]
            q_stack = jnp.concatenate([qh * grp_mask[g] for g in range(GROUP)], axis=0)
            s_all = lax.dot_general(q_stack, kt_scr[h, b * BLOCK:(b + 2) * BLOCK, :], contract_lanes,
                                    preferred_element_type=f32)
            probs = []
            for g in range(GROUP):
                s = jnp.where(valid, s_all[g * BLOCK:(g + 1) * BLOCK, :], -jnp.inf)
                sink = sinks_ref[h * GROUP + g]
                m = jnp.maximum(jnp.max(s, axis=-1, keepdims=True), sink)
                p = jnp.exp(s - m)
                denom = jnp.sum(p, axis=-1, keepdims=True) + jnp.exp(sink - m)
                probs.append((p * (1.0 / denom)).astype(bf16))
            o_all = jnp.dot(jnp.concatenate(probs, axis=0), vt_scr[h, b * BLOCK:(b + 2) * BLOCK, :],
                            preferred_element_type=f32)
            o = o_all[(GROUP - 1) * BLOCK:, :]
            for g in range(GROUP - 2, -1, -1):
                o = jnp.where(lane_grp == g, o_all[g * BLOCK:(g + 1) * BLOCK, :], o)
            c_out[b * BLOCK:(b + 1) * BLOCK, h * GROUP * HEAD_DIM:(h + 1) * GROUP * HEAD_DIM] = (
                o.astype(c_out.dtype))

    first = i % tiles_per_seq == 0
    keep = jnp.where(first, 0.0, 1.0).astype(f32)
    for h in range(N_KV_HEADS):
        kt_scr[h, 0:BLOCK, :] = (kt_scr[h, tm:tm + BLOCK, :].astype(f32) * keep).astype(bf16)
        vt_scr[h, 0:BLOCK, :] = (vt_scr[h, tm:tm + BLOCK, :].astype(f32) * keep).astype(bf16)

    kv = jnp.dot(h_ref[...], wbf_scr[:, Q_WIDTH:], preferred_element_type=f32)
    for r in range(n_blk):
        rr = slice(r * BLOCK, (r + 1) * BLOCK)
        fill_tiled(kt_scr, kv[rr, 0:KV_WIDTH], (r + 1) * BLOCK)
        fill_tiled(vt_scr, kv[rr, KV_WIDTH:], (r + 1) * BLOCK)
    q = jnp.dot(h_ref[...], wbf_scr[:, :Q_WIDTH], preferred_element_type=f32)
    q_scr[...] = (q * (HEAD_DIM ** -0.5)).astype(bf16)
    for b in range(n_blk):
        attend(b, jnp.logical_not(first) if b == 0 else True)


def _attn_branch(h, w_in, layer, sinks, *, seq, tm=BRANCH_TM):
    t, k = h.shape
    n_tiles = t // tm
    kv_scr = pltpu.VMEM((N_KV_HEADS, tm + BLOCK, GROUP * HEAD_DIM), jnp.bfloat16)
    q_scr = pltpu.VMEM((tm, Q_WIDTH), jnp.bfloat16)
    return pl.pallas_call(
        functools.partial(_attn_branch_kernel, tm=tm, tiles_per_seq=seq // tm),
        grid=(n_tiles,),
        in_specs=[pl.BlockSpec(memory_space=pltpu.SMEM),
                  pl.BlockSpec((tm, k), lambda i: (i, 0)),
                  pl.BlockSpec((pl.Element(k), pl.Element(QKV_WIDTH)), lambda i: (layer * k, OFF_Q),
                               pipeline_mode=pl.Buffered(1))],
        out_specs=pl.BlockSpec((tm, Q_WIDTH), lambda i: (i, 0)),
        out_shape=jax.ShapeDtypeStruct((t, Q_WIDTH), jnp.bfloat16),
        scratch_shapes=[pltpu.VMEM((k, QKV_WIDTH), jnp.bfloat16), q_scr, kv_scr, kv_scr],
        compiler_params=_params(("arbitrary",)),
        name="attn_branch",
    )(sinks, h, w_in.reshape(-1, w_in.shape[2]))


def _gated_proj_kernel(a_ref, c_ref, bg_ref, cg_ref, bh_ref, hcg_ref, hbh_ref, sw_ref,
                       wa_ref, wb_ref, wc_ref, g0_ref, g1_ref, g2_ref, o_ref,
                       wa_bf, wb_bf, wc_bf, cb_scr, *, tm, tiles_per_seq):
    f32 = jnp.float32
    i = pl.program_id(1)

    @pl.when(i == 0)
    def _():
        _cast_weight(wa_ref, wa_bf)
        _cast_weight(wb_ref, wb_bf)
        _cast_weight(wc_ref, wc_bf)

    keep = jnp.where(i % tiles_per_seq == 0, 0.0, 1.0).astype(f32)
    cb_scr[0:SCONV_HALO, :] = hcg_ref[...].astype(f32) * hbh_ref[...].astype(f32) * keep
    cb_scr[SCONV_HALO:, :] = cg_ref[...].astype(f32) * bh_ref[...].astype(f32)
    sc = jnp.zeros((tm, SCONV_WIDTH), f32)
    for j in range(SCONV_KERNEL):
        start = SCONV_HALO - (SCONV_KERNEL - 1) + j
        sc = sc + sw_ref[pl.ds(j, 1), :] * cb_scr[pl.ds(start, tm), :]
    b_act = (bg_ref[...].astype(f32) * sc).astype(jnp.bfloat16)

    ya = jnp.dot(a_ref[...], wa_bf[...], preferred_element_type=f32)
    yb = jnp.dot(b_act, wb_bf[...], preferred_element_type=f32)
    yc = jnp.dot(c_ref[...], wc_bf[...], preferred_element_type=f32)
    merged = (g0_ref[...].astype(f32) * ya + g1_ref[...].astype(f32) * yb
              + g2_ref[...].astype(f32) * yc)
    o_ref[...] = merged.astype(o_ref.dtype)


def _gated_proj(a, c, u_s, gates, sconv_w, wa, wb, wc, layer, *, seq, tm=256, tn=D_MODEL):
    t, kdim = a.shape
    sw = SCONV_WIDTH
    per_gate = D_MODEL // tn
    halo_blk = tm // SCONV_HALO
    act = pl.BlockSpec((tm, kdim), lambda j, i: (i, 0))
    wsp = pl.BlockSpec((None, kdim, tn), lambda j, i: (layer, 0, j), pipeline_mode=pl.Buffered(1))
    wscr = pltpu.VMEM((kdim, tn), jnp.bfloat16)

    def ucol(cb):
        return pl.BlockSpec((tm, sw), lambda j, i: (i, cb))

    def uhalo(cb):
        return pl.BlockSpec((SCONV_HALO, sw), lambda j, i: (jnp.maximum(i * halo_blk - 1, 0), cb))

    def gate(bi):
        return pl.BlockSpec((tm, tn), lambda j, i: (i, bi * per_gate + j))

    return pl.pallas_call(
        functools.partial(_gated_proj_kernel, tm=tm, tiles_per_seq=seq // tm),
        grid=(D_MODEL // tn, t // tm),
        in_specs=[act, act, ucol(0), ucol(1), ucol(2), uhalo(1), uhalo(2),
                  pl.BlockSpec((SCONV_KERNEL, sw), lambda j, i: (0, 0)),
                  wsp, wsp, wsp, gate(0), gate(1), gate(2)],
        out_specs=pl.BlockSpec((tm, tn), lambda j, i: (i, j)),
        out_shape=jax.ShapeDtypeStruct((t, D_MODEL), jnp.bfloat16),
        scratch_shapes=[wscr, wscr, wscr, pltpu.VMEM((tm + SCONV_HALO, sw), jnp.float32)],
        compiler_params=_params(("arbitrary", "arbitrary")),
        name="gated_proj",
    )(a, c, u_s, u_s, u_s, u_s, u_s, sconv_w, wa, wb, wc, gates, gates, gates)


def _mm_res_norm_kernel(a_ref, w_ref, x_ref, g_ref, xo_ref, ho_ref):
    k = pl.program_id(1)

    @pl.when(k == 0)
    def _():
        xo_ref[...] = x_ref[...]

    xo_ref[...] += jnp.dot(a_ref[...], w_ref[...], preferred_element_type=jnp.float32)

    @pl.when(k == pl.num_programs(1) - 1)
    def _():
        ho_ref[...] = _rmsnorm_rows(xo_ref[...], g_ref[...]).astype(ho_ref.dtype)


def _mm_res_norm(a, w, x, g, *, tm, tk, norm_dtype, name):
    m, kdim = a.shape
    n = w.shape[1]
    return pl.pallas_call(
        _mm_res_norm_kernel,
        grid=(m // tm, kdim // tk),
        in_specs=[pl.BlockSpec((tm, tk), lambda i, k: (i, k)),
                  pl.BlockSpec((tk, n), lambda i, k: (k, 0)),
                  pl.BlockSpec((tm, n), lambda i, k: (i, 0)),
                  pl.BlockSpec((1, n), lambda i, k: (0, 0))],
        out_specs=[pl.BlockSpec((tm, n), lambda i, k: (i, 0)),
                   pl.BlockSpec((tm, n), lambda i, k: (i, 0))],
        out_shape=[jax.ShapeDtypeStruct((m, n), jnp.float32),
                   jax.ShapeDtypeStruct((m, n), norm_dtype)],
        compiler_params=_params(("parallel", "arbitrary")),
        name=name,
    )(a, w, x, g.reshape(1, n))


def _mm_res_norm_cols_kernel(a_ref, w_ref, x_ref, g_ref, xo_ref, ho_ref, *, tn):
    j = pl.program_id(1)
    cols = pl.ds(pl.multiple_of(j * tn, tn), tn)
    xo_ref[:, cols] = x_ref[...] + jnp.dot(a_ref[...], w_ref[...], preferred_element_type=jnp.float32)

    @pl.when(j == pl.num_programs(1) - 1)
    def _():
        ho_ref[...] = _rmsnorm_rows(xo_ref[...], g_ref[...]).astype(ho_ref.dtype)


def _mm_res_norm_cols(a, w, x, g, *, tm, tn, norm_dtype, name):
    m, kdim = a.shape
    n = w.shape[1]
    return pl.pallas_call(
        functools.partial(_mm_res_norm_cols_kernel, tn=tn),
        grid=(m // tm, n // tn),
        in_specs=[pl.BlockSpec((tm, kdim), lambda i, j: (i, 0)),
                  pl.BlockSpec((kdim, tn), lambda i, j: (0, j)),
                  pl.BlockSpec((tm, tn), lambda i, j: (i, j)),
                  pl.BlockSpec((1, n), lambda i, j: (0, 0))],
        out_specs=[pl.BlockSpec((tm, n), lambda i, j: (i, 0)),
                   pl.BlockSpec((tm, n), lambda i, j: (i, 0))],
        out_shape=[jax.ShapeDtypeStruct((m, n), jnp.float32),
                   jax.ShapeDtypeStruct((m, n), norm_dtype)],
        compiler_params=_params(("parallel", "arbitrary")),
        name=name,
    )(a, w, x, g.reshape(1, n))


def kernel(x, norm_mix_g, w_in, gate_b, conf_dw, conf_ln_g, conf_ln_b, w_conf_out,
           sconv_w, w_sconv_out, sinks, w_attn_out, w_mix_out, norm_ffn_g,
           w_up, w_down, final_g):
    bsz, seq, d = x.shape
    depth = w_in.shape[0]
    t = bsz * seq
    bf16 = jnp.bfloat16
    xf = x.reshape(t, d)

    h = _rmsnorm(xf, norm_mix_g[0], bf16)
    for l in range(depth):
        a_act = _conf_branch(h, w_in, l, conf_dw[l], conf_ln_g[l], conf_ln_b[l], seq=seq)
        c_act = _attn_branch(h, w_in, l, sinks[l], seq=seq)
        u_s, w_mix_bf = _proj(h, w_in, l, OFF_B_GATE, 3 * SCONV_WIDTH, side=w_mix_out,
                              tm=1024, tn=1536, name="proj_sconv")
        gates = _proj(h, w_in, l, OFF_G, N_BRANCH * D_MODEL, gate_bias=gate_b[l],
                      tm=1024, tn=1536, name="proj_gates")
        merged = _gated_proj(a_act, c_act, u_s, gates, sconv_w[l], w_conf_out, w_sconv_out,
                             w_attn_out, l, seq=seq)
        xf, h2 = _mm_res_norm(merged, w_mix_bf, xf, norm_ffn_g[l],
                              tm=512, tk=D_MODEL, norm_dtype=bf16, name="mix_out")
        hmid, w_down_bf = _proj(h2, w_up, l, 0, D_FF, relu2=True, side=w_down,
                                tm=1024, tn=1024, name="mlp_up")
        last = l == depth - 1
        g_next = final_g if last else norm_mix_g[l + 1]
        xf, h = _mm_res_norm_cols(hmid, w_down_bf, xf, g_next,
                                  tm=512, tn=512, norm_dtype=jnp.float32 if last else bf16,
                                  name="mlp_down")
    return h.reshape(bsz, seq, d)
```

```python
import functools

import jax
import jax.numpy as jnp
from jax import lax
from jax.experimental import pallas as pl
from jax.experimental.pallas import tpu as pltpu

D_MODEL = 2048
CONF_WIDTH = 1024
CONF_KERNEL = 31
SCONV_WIDTH = 1024
SCONV_KERNEL = 3
HEAD_DIM = 64
N_Q_HEADS = 16
N_KV_HEADS = 4
GROUP = N_Q_HEADS // N_KV_HEADS
Q_WIDTH = N_Q_HEADS * HEAD_DIM
KV_WIDTH = N_KV_HEADS * HEAD_DIM
WINDOW = 128
BLOCK = 128
N_BRANCH = 3
D_FF = 4 * D_MODEL
RMS_EPS = 1e-6
LN_EPS = 1e-5

OFF_A_VAL = 0
OFF_A_GATE = OFF_A_VAL + CONF_WIDTH
OFF_B_GATE = OFF_A_GATE + CONF_WIDTH
OFF_C_GATE = OFF_B_GATE + SCONV_WIDTH
OFF_B_H = OFF_C_GATE + SCONV_WIDTH
OFF_Q = OFF_B_H + SCONV_WIDTH
OFF_K = OFF_Q + Q_WIDTH
OFF_V = OFF_K + KV_WIDTH
OFF_G = OFF_V + KV_WIDTH
IN_WIDTH = OFF_G + N_BRANCH * D_MODEL
QKV_WIDTH = Q_WIDTH + 2 * KV_WIDTH
assert GROUP * HEAD_DIM == 2 * BLOCK == KV_WIDTH

LANES = 128
CONF_HALO = 32
SCONV_HALO = 16
CAST_ROWS = 256
BRANCH_TM = 512
VMEM_LIMIT = 56 * 1024 * 1024


def _params(sem):
    return pltpu.CompilerParams(dimension_semantics=sem, vmem_limit_bytes=VMEM_LIMIT)


def _sigmoid(x):
    return 0.5 * jnp.tanh(0.5 * x) + 0.5


def _cast_weight(w_ref, wb_ref):
    def cast_rows(r, carry):
        rs = pl.ds(pl.multiple_of(r * CAST_ROWS, CAST_ROWS), CAST_ROWS)
        wb_ref[rs, :] = w_ref[rs, :].astype(wb_ref.dtype)
        return carry
    lax.fori_loop(0, w_ref.shape[0] // CAST_ROWS, cast_rows, 0)


def _rmsnorm_rows(x, g):
    ms = jnp.mean(x * x, axis=-1, keepdims=True)
    return x * lax.rsqrt(ms + RMS_EPS) * g


def _rmsnorm_kernel(x_ref, g_ref, o_ref):
    o_ref[...] = _rmsnorm_rows(x_ref[...], g_ref[...]).astype(o_ref.dtype)


def _rmsnorm(x, g, out_dtype, tm=512):
    t, d = x.shape
    return pl.pallas_call(
        _rmsnorm_kernel,
        grid=(t // tm,),
        in_specs=[pl.BlockSpec((tm, d), lambda i: (i, 0)),
                  pl.BlockSpec((1, d), lambda i: (0, 0))],
        out_specs=pl.BlockSpec((tm, d), lambda i: (i, 0)),
        out_shape=jax.ShapeDtypeStruct((t, d), out_dtype),
        compiler_params=_params(("parallel",)),
        name="rmsnorm",
    )(x, g.reshape(1, d))


def _proj_kernel(*refs, gate, relu2, n_side):
    refs = list(refs)
    a_ref, w_ref = refs[:2]
    del refs[:2]
    b_ref = refs.pop(0) if gate else None
    side_refs = [refs.pop(0) for _ in range(n_side)]
    o_ref = refs.pop(0)
    side_outs = [refs.pop(0) for _ in range(n_side)]
    (wb_ref,) = refs

    @pl.when(pl.program_id(1) == 0)
    def _():
        _cast_weight(w_ref, wb_ref)

    for side_ref, side_out in zip(side_refs, side_outs):
        side_out[...] = side_ref[...].astype(side_out.dtype)

    acc = jnp.dot(a_ref[...], wb_ref[...], preferred_element_type=jnp.float32)
    if gate:
        acc = _sigmoid(acc + b_ref[...])
    if relu2:
        acc = jnp.square(jnp.maximum(acc, 0.0))
    o_ref[...] = acc.astype(o_ref.dtype)


def _proj(a, w_stack, layer, col0, n, *, gate_bias=None, relu2=False, side=(), tm, tn, name):
    m, k = a.shape
    nj, ni = n // tn, m // tm
    in_specs = [pl.BlockSpec((tm, k), lambda j, i: (i, 0)),
                pl.BlockSpec((pl.Element(k), pl.Element(tn)),
                             lambda j, i: (layer * k, pl.multiple_of(j * tn + col0, LANES)))]
    operands = [a, w_stack.reshape(-1, w_stack.shape[2])]
    out_specs = [pl.BlockSpec((tm, tn), lambda j, i: (i, j))]
    out_shape = [jax.ShapeDtypeStruct((m, n), jnp.bfloat16)]
    if gate_bias is not None:
        in_specs.append(pl.BlockSpec((1, tn), lambda j, i: (0, j)))
        operands.append(gate_bias.reshape(1, n))
    blk0 = layer * (nj * ni)
    for s in side:
        _, s_rows, s_cols = s.shape
        rows_per = s_rows // (nj * ni)
        assert rows_per * nj * ni == s_rows and rows_per % 16 == 0
        in_specs.append(pl.BlockSpec((rows_per, s_cols), lambda j, i: (blk0 + j * ni + i, 0)))
        operands.append(s.reshape(-1, s_cols))
    for s in side:
        _, s_rows, s_cols = s.shape
        rows_per = s_rows // (nj * ni)
        out_specs.append(pl.BlockSpec((rows_per, s_cols), lambda j, i: (j * ni + i, 0)))
        out_shape.append(jax.ShapeDtypeStruct((s_rows, s_cols), jnp.bfloat16))
    return pl.pallas_call(
        functools.partial(_proj_kernel, gate=gate_bias is not None, relu2=relu2, n_side=len(side)),
        grid=(nj, ni),
        in_specs=in_specs,
        out_specs=out_specs,
        out_shape=out_shape,
        scratch_shapes=[pltpu.VMEM((k, tn), jnp.bfloat16)],
        compiler_params=_params(("arbitrary", "arbitrary")),
        name=name,
    )(*operands)


def _conf_branch_kernel(h_ref, w_ref, cw_ref, lng_ref, lnb_ref, a_out,
                        wbf_scr, glu_scr, conv_scr, *, tm, tiles_per_seq):
    f32, bf16 = jnp.float32, jnp.bfloat16
    i = pl.program_id(0)
    n_slab = CONF_WIDTH // LANES

    @pl.when(i == 0)
    def _():
        def cast_rows(r, carry):
            rs = pl.ds(pl.multiple_of(r * CAST_ROWS, CAST_ROWS), CAST_ROWS)
            for c in range(n_slab):
                wbf_scr[rs, (2 * c) * LANES:(2 * c + 1) * LANES] = (
                    w_ref[rs, c * LANES:(c + 1) * LANES].astype(bf16))
                wbf_scr[rs, (2 * c + 1) * LANES:(2 * c + 2) * LANES] = (
                    w_ref[rs, CONF_WIDTH + c * LANES:CONF_WIDTH + (c + 1) * LANES].astype(bf16))
            return carry
        lax.fori_loop(0, w_ref.shape[0] // CAST_ROWS, cast_rows, 0)
        glu_scr[...] = jnp.zeros_like(glu_scr)

    rows = 128
    half = rows // 2

    keep = jnp.where(i % tiles_per_seq == 0, 0.0, 1.0).astype(f32)
    for c in range(n_slab):
        glu_scr[c, 0:CONF_HALO, :] = glu_scr[c, tm:tm + CONF_HALO, :] * keep

    raw = jnp.dot(h_ref[...], wbf_scr[...], preferred_element_type=f32)
    for c in range(n_slab):
        val = raw[:, (2 * c) * LANES:(2 * c + 1) * LANES]
        gate = raw[:, (2 * c + 1) * LANES:(2 * c + 2) * LANES]
        glu_scr[c, pl.ds(CONF_HALO, tm), :] = val * _sigmoid(gate)
        lanes = slice(c * LANES, (c + 1) * LANES)
        for r in range(tm // rows):
            for par in range(2):
                acc = jnp.zeros((half, LANES), f32)
                for j in range(CONF_KERNEL):
                    start = r * rows + par + CONF_HALO - (CONF_KERNEL - 1) + j
                    acc = acc + cw_ref[pl.ds(j, 1), lanes] * glu_scr[c, pl.ds(start, half, stride=2), :]
                conv_scr[c, pl.ds(r * rows + par, half, stride=2), :] = acc

    def ln_body(r, carry):
        rs = pl.ds(pl.multiple_of(r * rows, rows), rows)
        y = jnp.concatenate([conv_scr[c, rs, :] for c in range(n_slab)], axis=-1)
        mu = jnp.mean(y, axis=-1, keepdims=True)
        yc = y - mu
        var = jnp.mean(yc * yc, axis=-1, keepdims=True)
        z = yc * lax.rsqrt(var + LN_EPS) * lng_ref[...] + lnb_ref[...]
        a_out[rs, :] = (z * _sigmoid(z)).astype(a_out.dtype)
        return carry
    lax.fori_loop(0, tm // rows, ln_body, 0)


def _conf_branch(h, w_in, layer, conf_dw, ln_g, ln_b, *, seq, tm=BRANCH_TM):
    t, k = h.shape
    n_tiles = t // tm
    cw = CONF_WIDTH
    return pl.pallas_call(
        functools.partial(_conf_branch_kernel, tm=tm, tiles_per_seq=seq // tm),
        grid=(n_tiles,),
        in_specs=[pl.BlockSpec((tm, k), lambda i: (i, 0)),
                  pl.BlockSpec((None, k, 2 * cw), lambda i: (layer, 0, 0),
                               pipeline_mode=pl.Buffered(1)),
                  pl.BlockSpec((CONF_KERNEL, cw), lambda i: (0, 0)),
                  pl.BlockSpec((1, cw), lambda i: (0, 0)),
                  pl.BlockSpec((1, cw), lambda i: (0, 0))],
        out_specs=pl.BlockSpec((tm, cw), lambda i: (i, 0)),
        out_shape=jax.ShapeDtypeStruct((t, cw), jnp.bfloat16),
        scratch_shapes=[
            pltpu.VMEM((k, 2 * cw), jnp.bfloat16),
            pltpu.VMEM((cw // LANES, tm + CONF_HALO, LANES), jnp.float32),
            pltpu.VMEM((cw // LANES, tm, LANES), jnp.float32),
        ],
        compiler_params=_params(("arbitrary",)),
        name="conf_branch",
    )(h, w_in, conf_dw, ln_g.reshape(1, cw), ln_b.reshape(1, cw))


def _attn_branch_kernel(sinks_ref, h_ref, w_ref, c_out,
                        wbf_scr, q_scr, kt_scr, vt_scr, *, tm, tiles_per_seq):
    f32, bf16 = jnp.float32, jnp.bfloat16
    i = pl.program_id(0)
    n_blk = tm // BLOCK

    @pl.when(i == 0)
    def _():
        _cast_weight(w_ref, wbf_scr)
        kt_scr[...] = jnp.zeros_like(kt_scr)
        vt_scr[...] = jnp.zeros_like(vt_scr)

    lane_lo = lax.broadcasted_iota(jnp.int32, (BLOCK, LANES), 1) < HEAD_DIM
    qi = lax.broadcasted_iota(jnp.int32, (BLOCK, 2 * BLOCK), 0)
    kj = lax.broadcasted_iota(jnp.int32, (BLOCK, 2 * BLOCK), 1)
    band = (kj > qi) & (kj <= qi + WINDOW)
    lane_grp = kj // HEAD_DIM
    grp_mask = [jnp.where(lane_grp == g, 1.0, 0.0).astype(bf16) for g in range(GROUP)]
    contract_lanes = (((1,), (1,)), ((), ()))

    def fill_tiled(dst_scr, src, row0):
        for pair in range(KV_WIDTH // LANES):
            both = src[:, pair * LANES:(pair + 1) * LANES]
            swapped = pltpu.roll(both, HEAD_DIM, axis=1)
            even = jnp.where(lane_lo, both, swapped).astype(bf16)
            odd = jnp.where(lane_lo, swapped, both).astype(bf16)
            for rep in range(KV_WIDTH // LANES):
                dst_scr[2 * pair, pl.ds(row0, BLOCK), rep * LANES:(rep + 1) * LANES] = even
                dst_scr[2 * pair + 1, pl.ds(row0, BLOCK), rep * LANES:(rep + 1) * LANES] = odd

    def attend(b, has_prev):
        valid = band & ((kj >= BLOCK) | has_prev)
        qblk = q_scr[b * BLOCK:(b + 1) * BLOCK, :]
        for h in range(N_KV_HEADS):
            qh = qblk[:, h * GROUP * HEAD_DIM:(h + 1) * GROUP * HEAD_DIM]
            q_stack = jnp.concatenate([qh * grp_mask[g] for g in range(GROUP)], axis=0)
            s_all = lax.dot_general(q_stack, kt_scr[h, b * BLOCK:(b + 2) * BLOCK, :], contract_lanes,
                                    preferred_element_type=f32)
            probs = []
            for g in range(GROUP):
                s = jnp.where(valid, s_all[g * BLOCK:(g + 1) * BLOCK, :], -jnp.inf)
                sink = sinks_ref[h * GROUP + g]
                m = jnp.maximum(jnp.max(s, axis=-1, keepdims=True), sink)
                p = jnp.exp(s - m)
                denom = jnp.sum(p, axis=-1, keepdims=True) + jnp.exp(sink - m)
                probs.append((p * (1.0 / denom)).astype(bf16))
            o_all = jnp.dot(jnp.concatenate(probs, axis=0), vt_scr[h, b * BLOCK:(b + 2) * BLOCK, :],
                            preferred_element_type=f32)
            o = o_all[(GROUP - 1) * BLOCK:, :]
            for g in range(GROUP - 2, -1, -1):
                o = jnp.where(lane_grp == g, o_all[g * BLOCK:(g + 1) * BLOCK, :], o)
            c_out[b * BLOCK:(b + 1) * BLOCK, h * GROUP * HEAD_DIM:(h + 1) * GROUP * HEAD_DIM] = (
                o.astype(c_out.dtype))

    first = i % tiles_per_seq == 0
    keep = jnp.where(first, 0.0, 1.0).astype(f32)
    for h in range(N_KV_HEADS):
        kt_scr[h, 0:BLOCK, :] = (kt_scr[h, tm:tm + BLOCK, :].astype(f32) * keep).astype(bf16)
        vt_scr[h, 0:BLOCK, :] = (vt_scr[h, tm:tm + BLOCK, :].astype(f32) * keep).astype(bf16)

    kv = jnp.dot(h_ref[...], wbf_scr[:, Q_WIDTH:], preferred_element_type=f32)
    for r in range(n_blk):
        rr = slice(r * BLOCK, (r + 1) * BLOCK)
        fill_tiled(kt_scr, kv[rr, 0:KV_WIDTH], (r + 1) * BLOCK)
        fill_tiled(vt_scr, kv[rr, KV_WIDTH:], (r + 1) * BLOCK)
    q = jnp.dot(h_ref[...], wbf_scr[:, :Q_WIDTH], preferred_element_type=f32)
    q_scr[...] = (q * (HEAD_DIM ** -0.5)).astype(bf16)
    for b in range(n_blk):
        attend(b, jnp.logical_not(first) if b == 0 else True)


def _attn_branch(h, w_in, layer, sinks, *, seq, tm=BRANCH_TM):
    t, k = h.shape
    n_tiles = t // tm
    kv_scr = pltpu.VMEM((N_KV_HEADS, tm + BLOCK, GROUP * HEAD_DIM), jnp.bfloat16)
    q_scr = pltpu.VMEM((tm, Q_WIDTH), jnp.bfloat16)
    return pl.pallas_call(
        functools.partial(_attn_branch_kernel, tm=tm, tiles_per_seq=seq // tm),
        grid=(n_tiles,),
        in_specs=[pl.BlockSpec(memory_space=pltpu.SMEM),
                  pl.BlockSpec((tm, k), lambda i: (i, 0)),
                  pl.BlockSpec((pl.Element(k), pl.Element(QKV_WIDTH)), lambda i: (layer * k, OFF_Q),
                               pipeline_mode=pl.Buffered(1))],
        out_specs=pl.BlockSpec((tm, Q_WIDTH), lambda i: (i, 0)),
        out_shape=jax.ShapeDtypeStruct((t, Q_WIDTH), jnp.bfloat16),
        scratch_shapes=[pltpu.VMEM((k, QKV_WIDTH), jnp.bfloat16), q_scr, kv_scr, kv_scr],
        compiler_params=_params(("arbitrary",)),
        name="attn_branch",
    )(sinks, h, w_in.reshape(-1, w_in.shape[2]))


def _merge_out_kernel(a_ref, c_ref, bg_ref, cg_ref, bh_ref, hcg_ref, hbh_ref, sw_ref,
                      wa_ref, wb_ref, wc_ref, wm_ref, g0_ref, g1_ref, g2_ref, x_ref, gn_ref,
                      xo_ref, ho_ref, cb_scr, *, tm, tiles_per_seq):
    f32 = jnp.float32
    i = pl.program_id(0)

    keep = jnp.where(i % tiles_per_seq == 0, 0.0, 1.0).astype(f32)
    cb_scr[0:SCONV_HALO, :] = hcg_ref[...].astype(f32) * hbh_ref[...].astype(f32) * keep
    cb_scr[SCONV_HALO:, :] = cg_ref[...].astype(f32) * bh_ref[...].astype(f32)
    sc = jnp.zeros((tm, SCONV_WIDTH), f32)
    for j in range(SCONV_KERNEL):
        start = SCONV_HALO - (SCONV_KERNEL - 1) + j
        sc = sc + sw_ref[pl.ds(j, 1), :] * cb_scr[pl.ds(start, tm), :]
    b_act = (bg_ref[...].astype(f32) * sc).astype(jnp.bfloat16)

    ya = jnp.dot(a_ref[...], wa_ref[...], preferred_element_type=f32)
    yb = jnp.dot(b_act, wb_ref[...], preferred_element_type=f32)
    yc = jnp.dot(c_ref[...], wc_ref[...], preferred_element_type=f32)
    merged = (g0_ref[...].astype(f32) * ya + g1_ref[...].astype(f32) * yb
              + g2_ref[...].astype(f32) * yc)
    x_new = x_ref[...] + jnp.dot(merged.astype(jnp.bfloat16), wm_ref[...], preferred_element_type=f32)
    xo_ref[...] = x_new
    ho_ref[...] = _rmsnorm_rows(x_new, gn_ref[...]).astype(ho_ref.dtype)


def _merge_out(a, c, u_s, gates, sconv_w, wa, wb, wc, wm, x, g_norm, *, seq, tm=256):
    t, kdim = a.shape
    d = x.shape[1]
    sw = SCONV_WIDTH
    halo_blk = tm // SCONV_HALO
    act = pl.BlockSpec((tm, kdim), lambda i: (i, 0))
    row = pl.BlockSpec((tm, d), lambda i: (i, 0))

    def weight(w):
        return pl.BlockSpec(w.shape, lambda i: (0, 0), pipeline_mode=pl.Buffered(1))

    def ucol(cb):
        return pl.BlockSpec((tm, sw), lambda i: (i, cb))

    def uhalo(cb):
        return pl.BlockSpec((SCONV_HALO, sw), lambda i: (jnp.maximum(i * halo_blk - 1, 0), cb))

    def gate(bi):
        return pl.BlockSpec((tm, d), lambda i: (i, bi))

    return pl.pallas_call(
        functools.partial(_merge_out_kernel, tm=tm, tiles_per_seq=seq // tm),
        grid=(t // tm,),
        in_specs=[act, act, ucol(0), ucol(1), ucol(2), uhalo(1), uhalo(2),
                  pl.BlockSpec((SCONV_KERNEL, sw), lambda i: (0, 0)),
                  weight(wa), weight(wb), weight(wc), weight(wm),
                  gate(0), gate(1), gate(2), row, pl.BlockSpec((1, d), lambda i: (0, 0))],
        out_specs=[row, row],
        out_shape=[jax.ShapeDtypeStruct((t, d), jnp.float32),
                   jax.ShapeDtypeStruct((t, d), jnp.bfloat16)],
        scratch_shapes=[pltpu.VMEM((tm + SCONV_HALO, sw), jnp.float32)],
        compiler_params=_params(("parallel",)),
        name="merge_out",
    )(a, c, u_s, u_s, u_s, u_s, u_s, sconv_w, wa, wb, wc, wm, gates, gates, gates, x,
      g_norm.reshape(1, d))


def _mm_res_norm_cols_kernel(a_ref, w_ref, x_ref, g_ref, xo_ref, ho_ref, *, tn):
    j = pl.program_id(1)
    cols = pl.ds(pl.multiple_of(j * tn, tn), tn)
    xo_ref[:, cols] = x_ref[...] + jnp.dot(a_ref[...], w_ref[...], preferred_element_type=jnp.float32)

    @pl.when(j == pl.num_programs(1) - 1)
    def _():
        ho_ref[...] = _rmsnorm_rows(xo_ref[...], g_ref[...]).astype(ho_ref.dtype)


def _mm_res_norm_cols(a, w, x, g, *, tm, tn, norm_dtype, name):
    m, kdim = a.shape
    n = w.shape[1]
    return pl.pallas_call(
        functools.partial(_mm_res_norm_cols_kernel, tn=tn),
        grid=(m // tm, n // tn),
        in_specs=[pl.BlockSpec((tm, kdim), lambda i, j: (i, 0)),
                  pl.BlockSpec((kdim, tn), lambda i, j: (0, j)),
                  pl.BlockSpec((tm, tn), lambda i, j: (i, j)),
                  pl.BlockSpec((1, n), lambda i, j: (0, 0))],
        out_specs=[pl.BlockSpec((tm, n), lambda i, j: (i, 0)),
                   pl.BlockSpec((tm, n), lambda i, j: (i, 0))],
        out_shape=[jax.ShapeDtypeStruct((m, n), jnp.float32),
                   jax.ShapeDtypeStruct((m, n), norm_dtype)],
        compiler_params=_params(("parallel", "arbitrary")),
        name=name,
    )(a, w, x, g.reshape(1, n))


def kernel(x, norm_mix_g, w_in, gate_b, conf_dw, conf_ln_g, conf_ln_b, w_conf_out,
           sconv_w, w_sconv_out, sinks, w_attn_out, w_mix_out, norm_ffn_g,
           w_up, w_down, final_g):
    bsz, seq, d = x.shape
    depth = w_in.shape[0]
    t = bsz * seq
    bf16 = jnp.bfloat16
    xf = x.reshape(t, d)

    h = _rmsnorm(xf, norm_mix_g[0], bf16)
    for l in range(depth):
        a_act = _conf_branch(h, w_in, l, conf_dw[l], conf_ln_g[l], conf_ln_b[l], seq=seq)
        c_act = _attn_branch(h, w_in, l, sinks[l], seq=seq)
        u_s, wm_bf = _proj(h, w_in, l, OFF_B_GATE, 3 * SCONV_WIDTH, side=(w_mix_out,),
                           tm=1024, tn=1536, name="proj_sconv")
        gates, wa_bf, wb_bf, wc_bf = _proj(h, w_in, l, OFF_G, N_BRANCH * D_MODEL, gate_bias=gate_b[l],
                                           side=(w_conf_out, w_sconv_out, w_attn_out),
                                           tm=1024, tn=1536, name="proj_gates")
        xf, h2 = _merge_out(a_act, c_act, u_s, gates, sconv_w[l], wa_bf, wb_bf, wc_bf, wm_bf,
                            xf, norm_ffn_g[l], seq=seq)
        hmid, w_down_bf = _proj(h2, w_up, l, 0, D_FF, relu2=True, side=(w_down,),
                                tm=1024, tn=1024, name="mlp_up")
        last = l == depth - 1
        g_next = final_g if last else norm_mix_g[l + 1]
        xf, h = _mm_res_norm_cols(hmid, w_down_bf, xf, g_next,
                                  tm=512, tn=512, norm_dtype=jnp.float32 if last else bf16,
                                  name="mlp_down")
    return h.reshape(bsz, seq, d)
```

```python
import functools

import jax
import jax.numpy as jnp
from jax import lax
from jax.experimental import pallas as pl
from jax.experimental.pallas import tpu as pltpu

D_MODEL = 2048
CONF_WIDTH = 1024
CONF_KERNEL = 31
SCONV_WIDTH = 1024
SCONV_KERNEL = 3
HEAD_DIM = 64
N_Q_HEADS = 16
N_KV_HEADS = 4
GROUP = N_Q_HEADS // N_KV_HEADS
Q_WIDTH = N_Q_HEADS * HEAD_DIM
KV_WIDTH = N_KV_HEADS * HEAD_DIM
WINDOW = 128
BLOCK = 128
N_BRANCH = 3
D_FF = 4 * D_MODEL
RMS_EPS = 1e-6
LN_EPS = 1e-5

OFF_A_VAL = 0
OFF_A_GATE = OFF_A_VAL + CONF_WIDTH
OFF_B_GATE = OFF_A_GATE + CONF_WIDTH
OFF_C_GATE = OFF_B_GATE + SCONV_WIDTH
OFF_B_H = OFF_C_GATE + SCONV_WIDTH
OFF_Q = OFF_B_H + SCONV_WIDTH
OFF_K = OFF_Q + Q_WIDTH
OFF_V = OFF_K + KV_WIDTH
OFF_G = OFF_V + KV_WIDTH
IN_WIDTH = OFF_G + N_BRANCH * D_MODEL
QKV_WIDTH = Q_WIDTH + 2 * KV_WIDTH
assert GROUP * HEAD_DIM == 2 * BLOCK == KV_WIDTH

LANES = 128
CONF_HALO = 32
SCONV_HALO = 16
CAST_ROWS = 256

V7X_VMEM_BYTES = 64 * 1024 * 1024
VMEM_LIMIT = V7X_VMEM_BYTES - 8 * 1024 * 1024
NORM_TM = 512
BRANCH_TM = 512
PROJ_TM, PROJ_TN = 1024, 1536
UP_TM, UP_TN = 2048, 1024
DOWN_TM, DOWN_TN = 512, 512
MERGE_TM = 256


def _params(sem):
    return pltpu.CompilerParams(dimension_semantics=sem, vmem_limit_bytes=VMEM_LIMIT)


def _sigmoid(x):
    return 0.5 * jnp.tanh(0.5 * x) + 0.5


def _cast_weight(w_ref, wb_ref):
    def cast_rows(r, carry):
        rs = pl.ds(pl.multiple_of(r * CAST_ROWS, CAST_ROWS), CAST_ROWS)
        wb_ref[rs, :] = w_ref[rs, :].astype(wb_ref.dtype)
        return carry
    lax.fori_loop(0, w_ref.shape[0] // CAST_ROWS, cast_rows, 0)


def _rmsnorm_rows(x, g):
    ms = jnp.mean(x * x, axis=-1, keepdims=True)
    return x * lax.rsqrt(ms + RMS_EPS) * g


def _rmsnorm_kernel(x_ref, g_ref, o_ref):
    o_ref[...] = _rmsnorm_rows(x_ref[...], g_ref[...]).astype(o_ref.dtype)


def _rmsnorm(x, g, out_dtype, tm=NORM_TM):
    t, d = x.shape
    return pl.pallas_call(
        _rmsnorm_kernel,
        grid=(t // tm,),
        in_specs=[pl.BlockSpec((tm, d), lambda i: (i, 0)),
                  pl.BlockSpec((1, d), lambda i: (0, 0))],
        out_specs=pl.BlockSpec((tm, d), lambda i: (i, 0)),
        out_shape=jax.ShapeDtypeStruct((t, d), out_dtype),
        compiler_params=_params(("parallel",)),
        name="rmsnorm",
    )(x, g.reshape(1, d))


def _proj_kernel(*refs, gate, relu2, n_side, precast):
    refs = list(refs)
    a_ref, w_ref = refs[:2]
    del refs[:2]
    b_ref = refs.pop(0) if gate else None
    side_refs = [refs.pop(0) for _ in range(n_side)]
    o_ref = refs.pop(0)
    side_outs = [refs.pop(0) for _ in range(n_side)]

    if precast:
        wb_ref = w_ref
    else:
        (wb_ref,) = refs

        @pl.when(pl.program_id(1) == 0)
        def _():
            _cast_weight(w_ref, wb_ref)

    for side_ref, side_out in zip(side_refs, side_outs):
        side_out[...] = side_ref[...].astype(side_out.dtype)

    acc = jnp.dot(a_ref[...], wb_ref[...], preferred_element_type=jnp.float32)
    if gate:
        acc = _sigmoid(acc + b_ref[...])
    if relu2:
        acc = jnp.square(jnp.maximum(acc, 0.0))
    o_ref[...] = acc.astype(o_ref.dtype)


def _proj(a, w_stack, layer, col0, n, *, gate_bias=None, relu2=False, side=(), tm, tn, name):
    m, k = a.shape
    nj, ni = n // tn, m // tm
    precast = w_stack.ndim == 2
    assert (w_stack.dtype == jnp.bfloat16) == precast
    row0 = 0 if precast else layer * k
    in_specs = [pl.BlockSpec((tm, k), lambda j, i: (i, 0)),
                pl.BlockSpec((pl.Element(k), pl.Element(tn)),
                             lambda j, i: (row0, pl.multiple_of(j * tn + col0, LANES)))]
    operands = [a, w_stack.reshape(-1, w_stack.shape[-1])]
    out_specs = [pl.BlockSpec((tm, tn), lambda j, i: (i, j))]
    out_shape = [jax.ShapeDtypeStruct((m, n), jnp.bfloat16)]
    if gate_bias is not None:
        in_specs.append(pl.BlockSpec((1, tn), lambda j, i: (0, j)))
        operands.append(gate_bias.reshape(1, n))
    blk0 = layer * (nj * ni)
    for s in side:
        _, s_rows, s_cols = s.shape
        rows_per = s_rows // (nj * ni)
        assert rows_per * nj * ni == s_rows and rows_per % 16 == 0
        in_specs.append(pl.BlockSpec((rows_per, s_cols), lambda j, i: (blk0 + j * ni + i, 0)))
        operands.append(s.reshape(-1, s_cols))
    for s in side:
        _, s_rows, s_cols = s.shape
        rows_per = s_rows // (nj * ni)
        out_specs.append(pl.BlockSpec((rows_per, s_cols), lambda j, i: (j * ni + i, 0)))
        out_shape.append(jax.ShapeDtypeStruct((s_rows, s_cols), jnp.bfloat16))
    return pl.pallas_call(
        functools.partial(_proj_kernel, gate=gate_bias is not None, relu2=relu2, n_side=len(side),
                          precast=precast),
        grid=(nj, ni),
        in_specs=in_specs,
        out_specs=out_specs,
        out_shape=out_shape,
        scratch_shapes=[] if precast else [pltpu.VMEM((k, tn), jnp.bfloat16)],
        compiler_params=_params(("arbitrary", "arbitrary")),
        name=name,
    )(*operands)


def _conf_branch_kernel(h_ref, w_ref, cw_ref, lng_ref, lnb_ref, side_ref, a_out, side_out,
                        wbf_scr, glu_scr, conv_scr, *, tm, tiles_per_seq):
    f32, bf16 = jnp.float32, jnp.bfloat16
    i = pl.program_id(0)
    n_slab = CONF_WIDTH // LANES

    side_out[...] = side_ref[...].astype(side_out.dtype)

    @pl.when(i == 0)
    def _():
        def cast_rows(r, carry):
            rs = pl.ds(pl.multiple_of(r * CAST_ROWS, CAST_ROWS), CAST_ROWS)
            for c in range(n_slab):
                wbf_scr[rs, (2 * c) * LANES:(2 * c + 1) * LANES] = (
                    w_ref[rs, c * LANES:(c + 1) * LANES].astype(bf16))
                wbf_scr[rs, (2 * c + 1) * LANES:(2 * c + 2) * LANES] = (
                    w_ref[rs, CONF_WIDTH + c * LANES:CONF_WIDTH + (c + 1) * LANES].astype(bf16))
            return carry
        lax.fori_loop(0, w_ref.shape[0] // CAST_ROWS, cast_rows, 0)
        glu_scr[...] = jnp.zeros_like(glu_scr)

    rows = 128
    half = rows // 2

    keep = jnp.where(i % tiles_per_seq == 0, 0.0, 1.0).astype(f32)
    for c in range(n_slab):
        glu_scr[c, 0:CONF_HALO, :] = glu_scr[c, tm:tm + CONF_HALO, :] * keep

    raw = jnp.dot(h_ref[...], wbf_scr[...], preferred_element_type=f32)
    for c in range(n_slab):
        val = raw[:, (2 * c) * LANES:(2 * c + 1) * LANES]
        gate = raw[:, (2 * c + 1) * LANES:(2 * c + 2) * LANES]
        glu_scr[c, pl.ds(CONF_HALO, tm), :] = val * _sigmoid(gate)
        lanes = slice(c * LANES, (c + 1) * LANES)
        for r in range(tm // rows):
            for par in range(2):
                acc = jnp.zeros((half, LANES), f32)
                for j in range(CONF_KERNEL):
                    start = r * rows + par + CONF_HALO - (CONF_KERNEL - 1) + j
                    acc = acc + cw_ref[pl.ds(j, 1), lanes] * glu_scr[c, pl.ds(start, half, stride=2), :]
                conv_scr[c, pl.ds(r * rows + par, half, stride=2), :] = acc

    def ln_body(r, carry):
        rs = pl.ds(pl.multiple_of(r * rows, rows), rows)
        y = jnp.concatenate([conv_scr[c, rs, :] for c in range(n_slab)], axis=-1)
        mu = jnp.mean(y, axis=-1, keepdims=True)
        yc = y - mu
        var = jnp.mean(yc * yc, axis=-1, keepdims=True)
        z = yc * lax.rsqrt(var + LN_EPS) * lng_ref[...] + lnb_ref[...]
        a_out[rs, :] = (z * _sigmoid(z)).astype(a_out.dtype)
        return carry
    lax.fori_loop(0, tm // rows, ln_body, 0)


def _conf_branch(h, w_in, layer, conf_dw, ln_g, ln_b, side, *, seq, tm=BRANCH_TM):
    t, k = h.shape
    n_tiles = t // tm
    cw = CONF_WIDTH
    _, s_rows, s_cols = side.shape
    rows_per = s_rows // n_tiles
    assert rows_per * n_tiles == s_rows and rows_per % 16 == 0
    return pl.pallas_call(
        functools.partial(_conf_branch_kernel, tm=tm, tiles_per_seq=seq // tm),
        grid=(n_tiles,),
        in_specs=[pl.BlockSpec((tm, k), lambda i: (i, 0)),
                  pl.BlockSpec((None, k, 2 * cw), lambda i: (layer, 0, 0),
                               pipeline_mode=pl.Buffered(1)),
                  pl.BlockSpec((CONF_KERNEL, cw), lambda i: (0, 0)),
                  pl.BlockSpec((1, cw), lambda i: (0, 0)),
                  pl.BlockSpec((1, cw), lambda i: (0, 0)),
                  pl.BlockSpec((rows_per, s_cols), lambda i: (layer * n_tiles + i, 0))],
        out_specs=[pl.BlockSpec((tm, cw), lambda i: (i, 0)),
                   pl.BlockSpec((rows_per, s_cols), lambda i: (i, 0))],
        out_shape=[jax.ShapeDtypeStruct((t, cw), jnp.bfloat16),
                   jax.ShapeDtypeStruct((s_rows, s_cols), jnp.bfloat16)],
        scratch_shapes=[
            pltpu.VMEM((k, 2 * cw), jnp.bfloat16),
            pltpu.VMEM((cw // LANES, tm + CONF_HALO, LANES), jnp.float32),
            pltpu.VMEM((cw // LANES, tm, LANES), jnp.float32),
        ],
        compiler_params=_params(("arbitrary",)),
        name="conf_branch",
    )(h, w_in, conf_dw, ln_g.reshape(1, cw), ln_b.reshape(1, cw), side.reshape(-1, s_cols))


def _attn_branch_kernel(sinks_ref, h_ref, w_ref, c_out,
                        wbf_scr, q_scr, kt_scr, vt_scr, *, tm, tiles_per_seq):
    f32, bf16 = jnp.float32, jnp.bfloat16
    i = pl.program_id(0)
    n_blk = tm // BLOCK

    @pl.when(i == 0)
    def _():
        _cast_weight(w_ref, wbf_scr)
        kt_scr[...] = jnp.zeros_like(kt_scr)
        vt_scr[...] = jnp.zeros_like(vt_scr)

    lane_lo = lax.broadcasted_iota(jnp.int32, (BLOCK, LANES), 1) < HEAD_DIM
    qi = lax.broadcasted_iota(jnp.int32, (BLOCK, 2 * BLOCK), 0)
    kj = lax.broadcasted_iota(jnp.int32, (BLOCK, 2 * BLOCK), 1)
    band = (kj > qi) & (kj <= qi + WINDOW)
    lane_grp = kj // HEAD_DIM
    grp_mask = [jnp.where(lane_grp == g, 1.0, 0.0).astype(bf16) for g in range(GROUP)]
    contract_lanes = (((1,), (1,)), ((), ()))

    def fill_tiled(dst_scr, src, row0):
        for pair in range(KV_WIDTH // LANES):
            both = src[:, pair * LANES:(pair + 1) * LANES]
            swapped = pltpu.roll(both, HEAD_DIM, axis=1)
            even = jnp.where(lane_lo, both, swapped).astype(bf16)
            odd = jnp.where(lane_lo, swapped, both).astype(bf16)
            for rep in range(KV_WIDTH // LANES):
                dst_scr[2 * pair, pl.ds(row0, BLOCK), rep * LANES:(rep + 1) * LANES] = even
                dst_scr[2 * pair + 1, pl.ds(row0, BLOCK), rep * LANES:(rep + 1) * LANES] = odd

    def attend(b, has_prev):
        valid = band & ((kj >= BLOCK) | has_prev)
        qblk = q_scr[b * BLOCK:(b + 1) * BLOCK, :]
        for h in range(N_KV_HEADS):
            qh = qblk[:, h * GROUP * HEAD_DIM:(h + 1) * GROUP * HEAD_DIM]
            q_stack = jnp.concatenate([qh * grp_mask[g] for g in range(GROUP)], axis=0)
            s_all = lax.dot_general(q_stack, kt_scr[h, b * BLOCK:(b + 2) * BLOCK, :], contract_lanes,
                                    preferred_element_type=f32)
            probs = []
            for g in range(GROUP):
                s = jnp.where(valid, s_all[g * BLOCK:(g + 1) * BLOCK, :], -jnp.inf)
                sink = sinks_ref[h * GROUP + g]
                m = jnp.maximum(jnp.max(s, axis=-1, keepdims=True), sink)
                p = jnp.exp(s - m)
                denom = jnp.sum(p, axis=-1, keepdims=True) + jnp.exp(sink - m)
                probs.append((p * (1.0 / denom)).astype(bf16))
            o_all = jnp.dot(jnp.concatenate(probs, axis=0), vt_scr[h, b * BLOCK:(b + 2) * BLOCK, :],
                            preferred_element_type=f32)
            o = o_all[(GROUP - 1) * BLOCK:, :]
            for g in range(GROUP - 2, -1, -1):
                o = jnp.where(lane_grp == g, o_all[g * BLOCK:(g + 1) * BLOCK, :], o)
            c_out[b * BLOCK:(b + 1) * BLOCK, h * GROUP * HEAD_DIM:(h + 1) * GROUP * HEAD_DIM] = (
                o.astype(c_out.dtype))

    first = i % tiles_per_seq == 0
    keep = jnp.where(first, 0.0, 1.0).astype(f32)
    for h in range(N_KV_HEADS):
        kt_scr[h, 0:BLOCK, :] = (kt_scr[h, tm:tm + BLOCK, :].astype(f32) * keep).astype(bf16)
        vt_scr[h, 0:BLOCK, :] = (vt_scr[h, tm:tm + BLOCK, :].astype(f32) * keep).astype(bf16)

    kv = jnp.dot(h_ref[...], wbf_scr[:, Q_WIDTH:], preferred_element_type=f32)
    for r in range(n_blk):
        rr = slice(r * BLOCK, (r + 1) * BLOCK)
        fill_tiled(kt_scr, kv[rr, 0:KV_WIDTH], (r + 1) * BLOCK)
        fill_tiled(vt_scr, kv[rr, KV_WIDTH:], (r + 1) * BLOCK)
    q = jnp.dot(h_ref[...], wbf_scr[:, :Q_WIDTH], preferred_element_type=f32)
    q_scr[...] = (q * (HEAD_DIM ** -0.5)).astype(bf16)
    for b in range(n_blk):
        attend(b, jnp.logical_not(first) if b == 0 else True)


def _attn_branch(h, w_in, layer, sinks, *, seq, tm=BRANCH_TM):
    t, k = h.shape
    n_tiles = t // tm
    kv_scr = pltpu.VMEM((N_KV_HEADS, tm + BLOCK, GROUP * HEAD_DIM), jnp.bfloat16)
    q_scr = pltpu.VMEM((tm, Q_WIDTH), jnp.bfloat16)
    return pl.pallas_call(
        functools.partial(_attn_branch_kernel, tm=tm, tiles_per_seq=seq // tm),
        grid=(n_tiles,),
        in_specs=[pl.BlockSpec(memory_space=pltpu.SMEM),
                  pl.BlockSpec((tm, k), lambda i: (i, 0)),
                  pl.BlockSpec((pl.Element(k), pl.Element(QKV_WIDTH)), lambda i: (layer * k, OFF_Q),
                               pipeline_mode=pl.Buffered(1))],
        out_specs=pl.BlockSpec((tm, Q_WIDTH), lambda i: (i, 0)),
        out_shape=jax.ShapeDtypeStruct((t, Q_WIDTH), jnp.bfloat16),
        scratch_shapes=[pltpu.VMEM((k, QKV_WIDTH), jnp.bfloat16), q_scr, kv_scr, kv_scr],
        compiler_params=_params(("arbitrary",)),
        name="attn_branch",
    )(sinks, h, w_in.reshape(-1, w_in.shape[2]))


def _merge_out_kernel(a_ref, c_ref, bg_ref, cg_ref, bh_ref, hcg_ref, hbh_ref, sw_ref,
                      wa_ref, wb_ref, wc_ref, wm_ref, g0_ref, g1_ref, g2_ref, x_ref, gn_ref,
                      xo_ref, ho_ref, cb_scr, *, tm, tiles_per_seq):
    f32 = jnp.float32
    i = pl.program_id(0)

    keep = jnp.where(i % tiles_per_seq == 0, 0.0, 1.0).astype(f32)
    cb_scr[0:SCONV_HALO, :] = hcg_ref[...].astype(f32) * hbh_ref[...].astype(f32) * keep
    cb_scr[SCONV_HALO:, :] = cg_ref[...].astype(f32) * bh_ref[...].astype(f32)
    sc = jnp.zeros((tm, SCONV_WIDTH), f32)
    for j in range(SCONV_KERNEL):
        start = SCONV_HALO - (SCONV_KERNEL - 1) + j
        sc = sc + sw_ref[pl.ds(j, 1), :] * cb_scr[pl.ds(start, tm), :]
    b_act = (bg_ref[...].astype(f32) * sc).astype(jnp.bfloat16)

    ya = jnp.dot(a_ref[...], wa_ref[...], preferred_element_type=f32)
    yb = jnp.dot(b_act, wb_ref[...], preferred_element_type=f32)
    yc = jnp.dot(c_ref[...], wc_ref[...], preferred_element_type=f32)
    merged = (g0_ref[...].astype(f32) * ya + g1_ref[...].astype(f32) * yb
              + g2_ref[...].astype(f32) * yc)
    x_new = x_ref[...] + jnp.dot(merged.astype(jnp.bfloat16), wm_ref[...], preferred_element_type=f32)
    xo_ref[...] = x_new
    ho_ref[...] = _rmsnorm_rows(x_new, gn_ref[...]).astype(ho_ref.dtype)


def _merge_out(a, c, u_s, gates, sconv_w, wa, wb, wc, wm, x, g_norm, *, seq, tm=MERGE_TM):
    t, kdim = a.shape
    d = x.shape[1]
    sw = SCONV_WIDTH
    halo_blk = tm // SCONV_HALO
    act = pl.BlockSpec((tm, kdim), lambda i: (i, 0))
    row = pl.BlockSpec((tm, d), lambda i: (i, 0))

    def weight(w):
        return pl.BlockSpec(w.shape, lambda i: (0, 0), pipeline_mode=pl.Buffered(1))

    def ucol(cb):
        return pl.BlockSpec((tm, sw), lambda i: (i, cb))

    def uhalo(cb):
        return pl.BlockSpec((SCONV_HALO, sw), lambda i: (jnp.maximum(i * halo_blk - 1, 0), cb))

    def gate(bi):
        return pl.BlockSpec((tm, d), lambda i: (i, bi))

    return pl.pallas_call(
        functools.partial(_merge_out_kernel, tm=tm, tiles_per_seq=seq // tm),
        grid=(t // tm,),
        in_specs=[act, act, ucol(0), ucol(1), ucol(2), uhalo(1), uhalo(2),
                  pl.BlockSpec((SCONV_KERNEL, sw), lambda i: (0, 0)),
                  weight(wa), weight(wb), weight(wc), weight(wm),
                  gate(0), gate(1), gate(2), row, pl.BlockSpec((1, d), lambda i: (0, 0))],
        out_specs=[row, row],
        out_shape=[jax.ShapeDtypeStruct((t, d), jnp.float32),
                   jax.ShapeDtypeStruct((t, d), jnp.bfloat16)],
        scratch_shapes=[pltpu.VMEM((tm + SCONV_HALO, sw), jnp.float32)],
        compiler_params=_params(("parallel",)),
        name="merge_out",
    )(a, c, u_s, u_s, u_s, u_s, u_s, sconv_w, wa, wb, wc, wm, gates, gates, gates, x,
      g_norm.reshape(1, d))


def _mm_res_norm_cols_kernel(a_ref, w_ref, x_ref, g_ref, xo_ref, ho_ref, *, tn):
    j = pl.program_id(1)
    cols = pl.ds(pl.multiple_of(j * tn, tn), tn)
    xo_ref[:, cols] = x_ref[...] + jnp.dot(a_ref[...], w_ref[...], preferred_element_type=jnp.float32)

    @pl.when(j == pl.num_programs(1) - 1)
    def _():
        ho_ref[...] = _rmsnorm_rows(xo_ref[...], g_ref[...]).astype(ho_ref.dtype)


def _mm_res_norm_cols(a, w, x, g, *, tm, tn, norm_dtype, name):
    m, kdim = a.shape
    n = w.shape[1]
    return pl.pallas_call(
        functools.partial(_mm_res_norm_cols_kernel, tn=tn),
        grid=(m // tm, n // tn),
        in_specs=[pl.BlockSpec((tm, kdim), lambda i, j: (i, 0)),
                  pl.BlockSpec((kdim, tn), lambda i, j: (0, j)),
                  pl.BlockSpec((tm, tn), lambda i, j: (i, j)),
                  pl.BlockSpec((1, n), lambda i, j: (0, 0))],
        out_specs=[pl.BlockSpec((tm, n), lambda i, j: (i, 0)),
                   pl.BlockSpec((tm, n), lambda i, j: (i, 0))],
        out_shape=[jax.ShapeDtypeStruct((m, n), jnp.float32),
                   jax.ShapeDtypeStruct((m, n), norm_dtype)],
        compiler_params=_params(("parallel", "arbitrary")),
        name=name,
    )(a, w, x, g.reshape(1, n))


def kernel(x, norm_mix_g, w_in, gate_b, conf_dw, conf_ln_g, conf_ln_b, w_conf_out,
           sconv_w, w_sconv_out, sinks, w_attn_out, w_mix_out, norm_ffn_g,
           w_up, w_down, final_g):
    bsz, seq, d = x.shape
    depth = w_in.shape[0]
    t = bsz * seq
    assert d == D_MODEL and w_in.shape[1:] == (D_MODEL, IN_WIDTH) and w_up.shape[1:] == (D_MODEL, D_FF)
    assert seq % BRANCH_TM == 0 and seq % MERGE_TM == 0 and BRANCH_TM % BLOCK == 0
    assert all(t % tm == 0 for tm in (NORM_TM, PROJ_TM, UP_TM, DOWN_TM))
    bf16 = jnp.bfloat16
    xf = x.reshape(t, d)

    h = _rmsnorm(xf, norm_mix_g[0], bf16)
    for l in range(depth):
        a_act, w_up_bf = _conf_branch(h, w_in, l, conf_dw[l], conf_ln_g[l], conf_ln_b[l], w_up, seq=seq)
        c_act = _attn_branch(h, w_in, l, sinks[l], seq=seq)
        u_s, wm_bf = _proj(h, w_in, l, OFF_B_GATE, 3 * SCONV_WIDTH, side=(w_mix_out,),
                           tm=PROJ_TM, tn=PROJ_TN, name="proj_sconv")
        gates, wa_bf, wb_bf, wc_bf = _proj(h, w_in, l, OFF_G, N_BRANCH * D_MODEL, gate_bias=gate_b[l],
                                           side=(w_conf_out, w_sconv_out, w_attn_out),
                                           tm=PROJ_TM, tn=PROJ_TN, name="proj_gates")
        xf, h2 = _merge_out(a_act, c_act, u_s, gates, sconv_w[l], wa_bf, wb_bf, wc_bf, wm_bf,
                            xf, norm_ffn_g[l], seq=seq)
        hmid, w_down_bf = _proj(h2, w_up_bf, l, 0, D_FF, relu2=True, side=(w_down,),
                                tm=UP_TM, tn=UP_TN, name="mlp_up")
        last = l == depth - 1
        g_next = final_g if last else norm_mix_g[l + 1]
        xf, h = _mm_res_norm_cols(hmid, w_down_bf, xf, g_next, tm=DOWN_TM, tn=DOWN_TN,
                                  norm_dtype=jnp.float32 if last else bf16, name="mlp_down")
    return h.reshape(bsz, seq, d)
```

```python
import functools

import jax
import jax.numpy as jnp
from jax import lax
from jax.experimental import pallas as pl
from jax.experimental.pallas import tpu as pltpu

D_MODEL = 2048
CONF_WIDTH = 1024
CONF_KERNEL = 31
SCONV_WIDTH = 1024
SCONV_KERNEL = 3
HEAD_DIM = 64
N_Q_HEADS = 16
N_KV_HEADS = 4
GROUP = N_Q_HEADS // N_KV_HEADS
Q_WIDTH = N_Q_HEADS * HEAD_DIM
KV_WIDTH = N_KV_HEADS * HEAD_DIM
WINDOW = 128
BLOCK = 128
N_BRANCH = 3
D_FF = 4 * D_MODEL
RMS_EPS = 1e-6
LN_EPS = 1e-5

OFF_A_VAL = 0
OFF_A_GATE = OFF_A_VAL + CONF_WIDTH
OFF_B_GATE = OFF_A_GATE + CONF_WIDTH
OFF_C_GATE = OFF_B_GATE + SCONV_WIDTH
OFF_B_H = OFF_C_GATE + SCONV_WIDTH
OFF_Q = OFF_B_H + SCONV_WIDTH
OFF_K = OFF_Q + Q_WIDTH
OFF_V = OFF_K + KV_WIDTH
OFF_G = OFF_V + KV_WIDTH
IN_WIDTH = OFF_G + N_BRANCH * D_MODEL
QKV_WIDTH = Q_WIDTH + 2 * KV_WIDTH
assert GROUP * HEAD_DIM == 2 * BLOCK == KV_WIDTH

LANES = 128
CONF_HALO = 32
SCONV_HALO = 16
CAST_ROWS = 256

V7X_VMEM_BYTES = 64 * 1024 * 1024
VMEM_LIMIT = V7X_VMEM_BYTES - 8 * 1024 * 1024
NORM_TM = 512
BRANCH_TM = 512
PROJ_TM, PROJ_TN = 1024, 1536
UP_TM, UP_TN = 2048, 1024
DOWN_TM, DOWN_TN = 512, 512
MERGE_TM = 256


def _params(sem):
    return pltpu.CompilerParams(dimension_semantics=sem, vmem_limit_bytes=VMEM_LIMIT)


def _sigmoid(x):
    return 0.5 * jnp.tanh(0.5 * x) + 0.5


def _history(rows, keep):
    return jnp.where(jnp.broadcast_to(keep, rows.shape) > 0.5, rows, jnp.zeros_like(rows))


def _cast_weight(w_ref, wb_ref):
    def cast_rows(r, carry):
        rs = pl.ds(pl.multiple_of(r * CAST_ROWS, CAST_ROWS), CAST_ROWS)
        wb_ref[rs, :] = w_ref[rs, :].astype(wb_ref.dtype)
        return carry
    lax.fori_loop(0, w_ref.shape[0] // CAST_ROWS, cast_rows, 0)


def _rmsnorm_rows(x, g):
    ms = jnp.mean(x * x, axis=-1, keepdims=True)
    return x * lax.rsqrt(ms + RMS_EPS) * g


def _rmsnorm_kernel(x_ref, g_ref, o_ref):
    o_ref[...] = _rmsnorm_rows(x_ref[...], g_ref[...]).astype(o_ref.dtype)


def _rmsnorm(x, g, out_dtype, tm=NORM_TM):
    t, d = x.shape
    return pl.pallas_call(
        _rmsnorm_kernel,
        grid=(t // tm,),
        in_specs=[pl.BlockSpec((tm, d), lambda i: (i, 0)),
                  pl.BlockSpec((1, d), lambda i: (0, 0))],
        out_specs=pl.BlockSpec((tm, d), lambda i: (i, 0)),
        out_shape=jax.ShapeDtypeStruct((t, d), out_dtype),
        compiler_params=_params(("parallel",)),
        name="rmsnorm",
    )(x, g.reshape(1, d))


def _proj_kernel(*refs, gate, relu2, n_side, precast):
    refs = list(refs)
    a_ref, w_ref = refs[:2]
    del refs[:2]
    b_ref = refs.pop(0) if gate else None
    side_refs = [refs.pop(0) for _ in range(n_side)]
    o_ref = refs.pop(0)
    side_outs = [refs.pop(0) for _ in range(n_side)]

    if precast:
        wb_ref = w_ref
    else:
        (wb_ref,) = refs

        @pl.when(pl.program_id(1) == 0)
        def _():
            _cast_weight(w_ref, wb_ref)

    for side_ref, side_out in zip(side_refs, side_outs):
        side_out[...] = side_ref[...].astype(side_out.dtype)

    acc = jnp.dot(a_ref[...], wb_ref[...], preferred_element_type=jnp.float32)
    if gate:
        acc = _sigmoid(acc + b_ref[...])
    if relu2:
        acc = jnp.square(jnp.maximum(acc, 0.0))
    o_ref[...] = acc.astype(o_ref.dtype)


def _proj(a, w_stack, layer, col0, n, *, gate_bias=None, relu2=False, side=(), tm, tn, name):
    m, k = a.shape
    nj, ni = n // tn, m // tm
    precast = w_stack.ndim == 2
    assert (w_stack.dtype == jnp.bfloat16) == precast
    row0 = 0 if precast else layer * k
    in_specs = [pl.BlockSpec((tm, k), lambda j, i: (i, 0)),
                pl.BlockSpec((pl.Element(k), pl.Element(tn)),
                             lambda j, i: (row0, pl.multiple_of(j * tn + col0, LANES)))]
    operands = [a, w_stack.reshape(-1, w_stack.shape[-1])]
    out_specs = [pl.BlockSpec((tm, tn), lambda j, i: (i, j))]
    out_shape = [jax.ShapeDtypeStruct((m, n), jnp.bfloat16)]
    if gate_bias is not None:
        in_specs.append(pl.BlockSpec((1, tn), lambda j, i: (0, j)))
        operands.append(gate_bias.reshape(1, n))
    blk0 = layer * (nj * ni)
    for s in side:
        _, s_rows, s_cols = s.shape
        rows_per = s_rows // (nj * ni)
        assert rows_per * nj * ni == s_rows and rows_per % 16 == 0
        in_specs.append(pl.BlockSpec((rows_per, s_cols), lambda j, i: (blk0 + j * ni + i, 0)))
        operands.append(s.reshape(-1, s_cols))
    for s in side:
        _, s_rows, s_cols = s.shape
        rows_per = s_rows // (nj * ni)
        out_specs.append(pl.BlockSpec((rows_per, s_cols), lambda j, i: (j * ni + i, 0)))
        out_shape.append(jax.ShapeDtypeStruct((s_rows, s_cols), jnp.bfloat16))
    return pl.pallas_call(
        functools.partial(_proj_kernel, gate=gate_bias is not None, relu2=relu2, n_side=len(side),
                          precast=precast),
        grid=(nj, ni),
        in_specs=in_specs,
        out_specs=out_specs,
        out_shape=out_shape,
        scratch_shapes=[] if precast else [pltpu.VMEM((k, tn), jnp.bfloat16)],
        compiler_params=_params(("arbitrary", "arbitrary")),
        name=name,
    )(*operands)


def _conf_branch_kernel(h_ref, w_ref, cw_ref, lng_ref, lnb_ref, side_ref, a_out, side_out,
                        wbf_scr, glu_scr, conv_scr, *, tm, tiles_per_seq):
    f32, bf16 = jnp.float32, jnp.bfloat16
    i = pl.program_id(0)
    n_slab = CONF_WIDTH // LANES

    side_out[...] = side_ref[...].astype(side_out.dtype)

    @pl.when(i == 0)
    def _():
        def cast_rows(r, carry):
            rs = pl.ds(pl.multiple_of(r * CAST_ROWS, CAST_ROWS), CAST_ROWS)
            for c in range(n_slab):
                wbf_scr[rs, (2 * c) * LANES:(2 * c + 1) * LANES] = (
                    w_ref[rs, c * LANES:(c + 1) * LANES].astype(bf16))
                wbf_scr[rs, (2 * c + 1) * LANES:(2 * c + 2) * LANES] = (
                    w_ref[rs, CONF_WIDTH + c * LANES:CONF_WIDTH + (c + 1) * LANES].astype(bf16))
            return carry
        lax.fori_loop(0, w_ref.shape[0] // CAST_ROWS, cast_rows, 0)
        glu_scr[...] = jnp.zeros_like(glu_scr)

    rows = 128
    half = rows // 2

    keep = jnp.where(i % tiles_per_seq == 0, 0.0, 1.0).astype(f32)
    for c in range(n_slab):
        glu_scr[c, 0:CONF_HALO, :] = _history(glu_scr[c, tm:tm + CONF_HALO, :], keep)

    raw = jnp.dot(h_ref[...], wbf_scr[...], preferred_element_type=f32)
    for c in range(n_slab):
        val = raw[:, (2 * c) * LANES:(2 * c + 1) * LANES]
        gate = raw[:, (2 * c + 1) * LANES:(2 * c + 2) * LANES]
        glu_scr[c, pl.ds(CONF_HALO, tm), :] = val * _sigmoid(gate)
        lanes = slice(c * LANES, (c + 1) * LANES)
        for r in range(tm // rows):
            for par in range(2):
                acc = jnp.zeros((half, LANES), f32)
                for j in range(CONF_KERNEL):
                    start = r * rows + par + CONF_HALO - (CONF_KERNEL - 1) + j
                    acc = acc + cw_ref[pl.ds(j, 1), lanes] * glu_scr[c, pl.ds(start, half, stride=2), :]
                conv_scr[c, pl.ds(r * rows + par, half, stride=2), :] = acc

    for r in range(tm // rows):
        rs = slice(r * rows, (r + 1) * rows)
        y = jnp.concatenate([conv_scr[c, rs, :] for c in range(n_slab)], axis=-1)
        mu = jnp.mean(y, axis=-1, keepdims=True)
        yc = y - mu
        var = jnp.mean(yc * yc, axis=-1, keepdims=True)
        z = yc * lax.rsqrt(var + LN_EPS) * lng_ref[...] + lnb_ref[...]
        a_out[rs, :] = (z * _sigmoid(z)).astype(a_out.dtype)


def _conf_branch(h, w_in, layer, conf_dw, ln_g, ln_b, side, *, seq, tm=BRANCH_TM):
    t, k = h.shape
    n_tiles = t // tm
    cw = CONF_WIDTH
    _, s_rows, s_cols = side.shape
    rows_per = s_rows // n_tiles
    assert rows_per * n_tiles == s_rows and rows_per % 16 == 0
    return pl.pallas_call(
        functools.partial(_conf_branch_kernel, tm=tm, tiles_per_seq=seq // tm),
        grid=(n_tiles,),
        in_specs=[pl.BlockSpec((tm, k), lambda i: (i, 0)),
                  pl.BlockSpec((None, k, 2 * cw), lambda i: (layer, 0, 0),
                               pipeline_mode=pl.Buffered(1)),
                  pl.BlockSpec((CONF_KERNEL, cw), lambda i: (0, 0)),
                  pl.BlockSpec((1, cw), lambda i: (0, 0)),
                  pl.BlockSpec((1, cw), lambda i: (0, 0)),
                  pl.BlockSpec((rows_per, s_cols), lambda i: (layer * n_tiles + i, 0))],
        out_specs=[pl.BlockSpec((tm, cw), lambda i: (i, 0)),
                   pl.BlockSpec((rows_per, s_cols), lambda i: (i, 0))],
        out_shape=[jax.ShapeDtypeStruct((t, cw), jnp.bfloat16),
                   jax.ShapeDtypeStruct((s_rows, s_cols), jnp.bfloat16)],
        scratch_shapes=[
            pltpu.VMEM((k, 2 * cw), jnp.bfloat16),
            pltpu.VMEM((cw // LANES, tm + CONF_HALO, LANES), jnp.float32),
            pltpu.VMEM((cw // LANES, tm, LANES), jnp.float32),
        ],
        compiler_params=_params(("arbitrary",)),
        name="conf_branch",
    )(h, w_in, conf_dw, ln_g.reshape(1, cw), ln_b.reshape(1, cw), side.reshape(-1, s_cols))


def _attn_branch_kernel(sinks_ref, h_ref, w_ref, c_out,
                        wbf_scr, q_scr, kt_scr, vt_scr, *, tm, tiles_per_seq):
    f32, bf16 = jnp.float32, jnp.bfloat16
    i = pl.program_id(0)
    n_blk = tm // BLOCK

    @pl.when(i == 0)
    def _():
        _cast_weight(w_ref, wbf_scr)
        kt_scr[...] = jnp.zeros_like(kt_scr)
        vt_scr[...] = jnp.zeros_like(vt_scr)

    lane_lo = lax.broadcasted_iota(jnp.int32, (BLOCK, LANES), 1) < HEAD_DIM
    qi = lax.broadcasted_iota(jnp.int32, (BLOCK, 2 * BLOCK), 0)
    kj = lax.broadcasted_iota(jnp.int32, (BLOCK, 2 * BLOCK), 1)
    band = (kj > qi) & (kj <= qi + WINDOW)
    lane_grp = kj // HEAD_DIM
    grp_mask = [jnp.where(lane_grp == g, 1.0, 0.0).astype(bf16) for g in range(GROUP)]
    contract_lanes = (((1,), (1,)), ((), ()))

    def fill_tiled(dst_scr, src, row0):
        for pair in range(KV_WIDTH // LANES):
            both = src[:, pair * LANES:(pair + 1) * LANES]
            swapped = pltpu.roll(both, HEAD_DIM, axis=1)
            even = jnp.where(lane_lo, both, swapped).astype(bf16)
            odd = jnp.where(lane_lo, swapped, both).astype(bf16)
            for rep in range(KV_WIDTH // LANES):
                dst_scr[2 * pair, pl.ds(row0, BLOCK), rep * LANES:(rep + 1) * LANES] = even
                dst_scr[2 * pair + 1, pl.ds(row0, BLOCK), rep * LANES:(rep + 1) * LANES] = odd

    def attend(b, has_prev):
        valid = band & ((kj >= BLOCK) | has_prev)
        qblk = q_scr[b * BLOCK:(b + 1) * BLOCK, :]
        for h in range(N_KV_HEADS):
            qh = qblk[:, h * GROUP * HEAD_DIM:(h + 1) * GROUP * HEAD_DIM]
            q_stack = jnp.concatenate([qh * grp_mask[g] for g in range(GROUP)], axis=0)
            s_all = lax.dot_general(q_stack, kt_scr[h, b * BLOCK:(b + 2) * BLOCK, :], contract_lanes,
                                    preferred_element_type=f32)
            probs = []
            for g in range(GROUP):
                s = jnp.where(valid, s_all[g * BLOCK:(g + 1) * BLOCK, :], -jnp.inf)
                sink = sinks_ref[h * GROUP + g]
                m = jnp.maximum(jnp.max(s, axis=-1, keepdims=True), sink)
                p = jnp.exp(s - m)
                denom = jnp.sum(p, axis=-1, keepdims=True) + jnp.exp(sink - m)
                probs.append((p * (1.0 / denom)).astype(bf16))
            o_all = jnp.dot(jnp.concatenate(probs, axis=0), vt_scr[h, b * BLOCK:(b + 2) * BLOCK, :],
                            preferred_element_type=f32)
            o = o_all[(GROUP - 1) * BLOCK:, :]
            for g in range(GROUP - 2, -1, -1):
                o = jnp.where(lane_grp == g, o_all[g * BLOCK:(g + 1) * BLOCK, :], o)
            c_out[b * BLOCK:(b + 1) * BLOCK, h * GROUP * HEAD_DIM:(h + 1) * GROUP * HEAD_DIM] = (
                o.astype(c_out.dtype))

    first = i % tiles_per_seq == 0
    keep = jnp.where(first, 0.0, 1.0).astype(f32)
    for h in range(N_KV_HEADS):
        kt_scr[h, 0:BLOCK, :] = _history(kt_scr[h, tm:tm + BLOCK, :].astype(f32), keep).astype(bf16)
        vt_scr[h, 0:BLOCK, :] = _history(vt_scr[h, tm:tm + BLOCK, :].astype(f32), keep).astype(bf16)

    kv = jnp.dot(h_ref[...], wbf_scr[:, Q_WIDTH:], preferred_element_type=f32)
    for r in range(n_blk):
        rr = slice(r * BLOCK, (r + 1) * BLOCK)
        fill_tiled(kt_scr, kv[rr, 0:KV_WIDTH], (r + 1) * BLOCK)
        fill_tiled(vt_scr, kv[rr, KV_WIDTH:], (r + 1) * BLOCK)
    q = jnp.dot(h_ref[...], wbf_scr[:, :Q_WIDTH], preferred_element_type=f32)
    q_scr[...] = (q * (HEAD_DIM ** -0.5)).astype(bf16)
    for b in range(n_blk):
        attend(b, jnp.logical_not(first) if b == 0 else True)


def _attn_branch(h, w_in, layer, sinks, *, seq, tm=BRANCH_TM):
    t, k = h.shape
    n_tiles = t // tm
    kv_scr = pltpu.VMEM((N_KV_HEADS, tm + BLOCK, GROUP * HEAD_DIM), jnp.bfloat16)
    q_scr = pltpu.VMEM((tm, Q_WIDTH), jnp.bfloat16)
    return pl.pallas_call(
        functools.partial(_attn_branch_kernel, tm=tm, tiles_per_seq=seq // tm),
        grid=(n_tiles,),
        in_specs=[pl.BlockSpec(memory_space=pltpu.SMEM),
                  pl.BlockSpec((tm, k), lambda i: (i, 0)),
                  pl.BlockSpec((pl.Element(k), pl.Element(QKV_WIDTH)), lambda i: (layer * k, OFF_Q),
                               pipeline_mode=pl.Buffered(1))],
        out_specs=pl.BlockSpec((tm, Q_WIDTH), lambda i: (i, 0)),
        out_shape=jax.ShapeDtypeStruct((t, Q_WIDTH), jnp.bfloat16),
        scratch_shapes=[pltpu.VMEM((k, QKV_WIDTH), jnp.bfloat16), q_scr, kv_scr, kv_scr],
        compiler_params=_params(("arbitrary",)),
        name="attn_branch",
    )(sinks, h, w_in.reshape(-1, w_in.shape[2]))


def _merge_out_kernel(a_ref, c_ref, bg_ref, cg_ref, bh_ref, hcg_ref, hbh_ref, sw_ref,
                      wa_ref, wb_ref, wc_ref, wm_ref, g0_ref, g1_ref, g2_ref, x_ref, gn_ref,
                      xo_ref, ho_ref, cb_scr, *, tm, tiles_per_seq):
    f32 = jnp.float32
    i = pl.program_id(0)

    keep = jnp.where(i % tiles_per_seq == 0, 0.0, 1.0).astype(f32)
    cb_scr[0:SCONV_HALO, :] = _history(hcg_ref[...].astype(f32) * hbh_ref[...].astype(f32), keep)
    cb_scr[SCONV_HALO:, :] = cg_ref[...].astype(f32) * bh_ref[...].astype(f32)
    sc = jnp.zeros((tm, SCONV_WIDTH), f32)
    for j in range(SCONV_KERNEL):
        start = SCONV_HALO - (SCONV_KERNEL - 1) + j
        sc = sc + sw_ref[pl.ds(j, 1), :] * cb_scr[pl.ds(start, tm), :]
    b_act = (bg_ref[...].astype(f32) * sc).astype(jnp.bfloat16)

    ya = jnp.dot(a_ref[...], wa_ref[...], preferred_element_type=f32)
    yb = jnp.dot(b_act, wb_ref[...], preferred_element_type=f32)
    yc = jnp.dot(c_ref[...], wc_ref[...], preferred_element_type=f32)
    merged = (g0_ref[...].astype(f32) * ya + g1_ref[...].astype(f32) * yb
              + g2_ref[...].astype(f32) * yc)
    x_new = x_ref[...] + jnp.dot(merged.astype(jnp.bfloat16), wm_ref[...], preferred_element_type=f32)
    xo_ref[...] = x_new
    ho_ref[...] = _rmsnorm_rows(x_new, gn_ref[...]).astype(ho_ref.dtype)


def _merge_out(a, c, u_s, gates, sconv_w, wa, wb, wc, wm, x, g_norm, *, seq, tm=MERGE_TM):
    t, kdim = a.shape
    d = x.shape[1]
    sw = SCONV_WIDTH
    halo_blk = tm // SCONV_HALO
    act = pl.BlockSpec((tm, kdim), lambda i: (i, 0))
    row = pl.BlockSpec((tm, d), lambda i: (i, 0))

    def weight(w):
        return pl.BlockSpec(w.shape, lambda i: (0, 0), pipeline_mode=pl.Buffered(1))

    def ucol(cb):
        return pl.BlockSpec((tm, sw), lambda i: (i, cb))

    def uhalo(cb):
        return pl.BlockSpec((SCONV_HALO, sw), lambda i: (jnp.maximum(i * halo_blk - 1, 0), cb))

    def gate(bi):
        return pl.BlockSpec((tm, d), lambda i: (i, bi))

    return pl.pallas_call(
        functools.partial(_merge_out_kernel, tm=tm, tiles_per_seq=seq // tm),
        grid=(t // tm,),
        in_specs=[act, act, ucol(0), ucol(1), ucol(2), uhalo(1), uhalo(2),
                  pl.BlockSpec((SCONV_KERNEL, sw), lambda i: (0, 0)),
                  weight(wa), weight(wb), weight(wc), weight(wm),
                  gate(0), gate(1), gate(2), row, pl.BlockSpec((1, d), lambda i: (0, 0))],
        out_specs=[row, row],
        out_shape=[jax.ShapeDtypeStruct((t, d), jnp.float32),
                   jax.ShapeDtypeStruct((t, d), jnp.bfloat16)],
        scratch_shapes=[pltpu.VMEM((tm + SCONV_HALO, sw), jnp.float32)],
        compiler_params=_params(("parallel",)),
        name="merge_out",
    )(a, c, u_s, u_s, u_s, u_s, u_s, sconv_w, wa, wb, wc, wm, gates, gates, gates, x,
      g_norm.reshape(1, d))


def _mm_res_norm_cols_kernel(a_ref, w_ref, x_ref, g_ref, xo_ref, ho_ref, *, tn):
    j = pl.program_id(1)
    cols = pl.ds(pl.multiple_of(j * tn, tn), tn)
    xo_ref[:, cols] = x_ref[...] + jnp.dot(a_ref[...], w_ref[...], preferred_element_type=jnp.float32)

    @pl.when(j == pl.num_programs(1) - 1)
    def _():
        ho_ref[...] = _rmsnorm_rows(xo_ref[...], g_ref[...]).astype(ho_ref.dtype)


def _mm_res_norm_cols(a, w, x, g, *, tm, tn, norm_dtype, name):
    m, kdim = a.shape
    n = w.shape[1]
    return pl.pallas_call(
        functools.partial(_mm_res_norm_cols_kernel, tn=tn),
        grid=(m // tm, n // tn),
        in_specs=[pl.BlockSpec((tm, kdim), lambda i, j: (i, 0)),
                  pl.BlockSpec((kdim, tn), lambda i, j: (0, j)),
                  pl.BlockSpec((tm, tn), lambda i, j: (i, j)),
                  pl.BlockSpec((1, n), lambda i, j: (0, 0))],
        out_specs=[pl.BlockSpec((tm, n), lambda i, j: (i, 0)),
                   pl.BlockSpec((tm, n), lambda i, j: (i, 0))],
        out_shape=[jax.ShapeDtypeStruct((m, n), jnp.float32),
                   jax.ShapeDtypeStruct((m, n), norm_dtype)],
        compiler_params=_params(("parallel", "arbitrary")),
        name=name,
    )(a, w, x, g.reshape(1, n))


def kernel(x, norm_mix_g, w_in, gate_b, conf_dw, conf_ln_g, conf_ln_b, w_conf_out,
           sconv_w, w_sconv_out, sinks, w_attn_out, w_mix_out, norm_ffn_g,
           w_up, w_down, final_g):
    bsz, seq, d = x.shape
    depth = w_in.shape[0]
    t = bsz * seq
    assert d == D_MODEL and w_in.shape[1:] == (D_MODEL, IN_WIDTH) and w_up.shape[1:] == (D_MODEL, D_FF)
    assert seq % BRANCH_TM == 0 and seq % MERGE_TM == 0 and BRANCH_TM % BLOCK == 0
    assert all(t % tm == 0 for tm in (NORM_TM, PROJ_TM, UP_TM, DOWN_TM))
    bf16 = jnp.bfloat16
    xf = x.reshape(t, d)

    h = _rmsnorm(xf, norm_mix_g[0], bf16)
    for l in range(depth):
        a_act, w_up_bf = _conf_branch(h, w_in, l, conf_dw[l], conf_ln_g[l], conf_ln_b[l], w_up, seq=seq)
        c_act = _attn_branch(h, w_in, l, sinks[l], seq=seq)
        u_s, wm_bf = _proj(h, w_in, l, OFF_B_GATE, 3 * SCONV_WIDTH, side=(w_mix_out,),
                           tm=PROJ_TM, tn=PROJ_TN, name="proj_sconv")
        gates, wa_bf, wb_bf, wc_bf = _proj(h, w_in, l, OFF_G, N_BRANCH * D_MODEL, gate_bias=gate_b[l],
                                           side=(w_conf_out, w_sconv_out, w_attn_out),
                                           tm=PROJ_TM, tn=PROJ_TN, name="proj_gates")
        xf, h2 = _merge_out(a_act, c_act, u_s, gates, sconv_w[l], wa_bf, wb_bf, wc_bf, wm_bf,
                            xf, norm_ffn_g[l], seq=seq)
        hmid, w_down_bf = _proj(h2, w_up_bf, l, 0, D_FF, relu2=True, side=(w_down,),
                                tm=UP_TM, tn=UP_TN, name="mlp_up")
        last = l == depth - 1
        g_next = final_g if last else norm_mix_g[l + 1]
        xf, h = _mm_res_norm_cols(hmid, w_down_bf, xf, g_next, tm=DOWN_TM, tn=DOWN_TN,
                                  norm_dtype=jnp.float32 if last else bf16, name="mlp_down")
    return h.reshape(bsz, seq, d)
```

```python
import functools

import jax
import jax.numpy as jnp
from jax import lax
from jax.experimental import pallas as pl
from jax.experimental.pallas import tpu as pltpu

D_MODEL = 2048
CONF_WIDTH = 1024
CONF_KERNEL = 31
SCONV_WIDTH = 1024
SCONV_KERNEL = 3
HEAD_DIM = 64
N_Q_HEADS = 16
N_KV_HEADS = 4
GROUP = N_Q_HEADS // N_KV_HEADS
Q_WIDTH = N_Q_HEADS * HEAD_DIM
KV_WIDTH = N_KV_HEADS * HEAD_DIM
WINDOW = 128
BLOCK = 128
N_BRANCH = 3
D_FF = 4 * D_MODEL
RMS_EPS = 1e-6
LN_EPS = 1e-5

OFF_A_VAL = 0
OFF_A_GATE = OFF_A_VAL + CONF_WIDTH
OFF_B_GATE = OFF_A_GATE + CONF_WIDTH
OFF_C_GATE = OFF_B_GATE + SCONV_WIDTH
OFF_B_H = OFF_C_GATE + SCONV_WIDTH
OFF_Q = OFF_B_H + SCONV_WIDTH
OFF_K = OFF_Q + Q_WIDTH
OFF_V = OFF_K + KV_WIDTH
OFF_G = OFF_V + KV_WIDTH
IN_WIDTH = OFF_G + N_BRANCH * D_MODEL
QKV_WIDTH = Q_WIDTH + 2 * KV_WIDTH
assert GROUP * HEAD_DIM == 2 * BLOCK == KV_WIDTH

LANES = 128
CONF_HALO = 32
SCONV_HALO = 16
CAST_ROWS = 256

V7X_VMEM_BYTES = 64 * 1024 * 1024
VMEM_LIMIT = V7X_VMEM_BYTES - 8 * 1024 * 1024
NORM_TM = 512
BRANCH_TM = 512
PROJ_TM, PROJ_TN = 1024, 1536
UP_TM, UP_TN = 2048, 1024
DOWN_TM, DOWN_TN = 512, 512
MERGE_TM = 256


def _params(sem):
    return pltpu.CompilerParams(dimension_semantics=sem, vmem_limit_bytes=VMEM_LIMIT)


def _sigmoid(x):
    return 0.5 * jnp.tanh(0.5 * x) + 0.5


def _history(rows, keep):
    return jnp.where(jnp.broadcast_to(keep, rows.shape) > 0.5, rows, jnp.zeros_like(rows))


def _cast_weight(w_ref, wb_ref):
    def cast_rows(r, carry):
        rs = pl.ds(pl.multiple_of(r * CAST_ROWS, CAST_ROWS), CAST_ROWS)
        wb_ref[rs, :] = w_ref[rs, :].astype(wb_ref.dtype)
        return carry
    lax.fori_loop(0, w_ref.shape[0] // CAST_ROWS, cast_rows, 0)


def _rmsnorm_rows(x, g):
    ms = jnp.mean(x * x, axis=-1, keepdims=True)
    return x * lax.rsqrt(ms + RMS_EPS) * g


def _rmsnorm_kernel(x_ref, g_ref, o_ref):
    o_ref[...] = _rmsnorm_rows(x_ref[...], g_ref[...]).astype(o_ref.dtype)


def _rmsnorm(x, g, out_dtype, tm=NORM_TM):
    t, d = x.shape
    return pl.pallas_call(
        _rmsnorm_kernel,
        grid=(t // tm,),
        in_specs=[pl.BlockSpec((tm, d), lambda i: (i, 0)),
                  pl.BlockSpec((1, d), lambda i: (0, 0))],
        out_specs=pl.BlockSpec((tm, d), lambda i: (i, 0)),
        out_shape=jax.ShapeDtypeStruct((t, d), out_dtype),
        compiler_params=_params(("parallel",)),
        name="rmsnorm",
    )(x, g.reshape(1, d))


def _proj_kernel(*refs, gate, relu2, n_side, precast):
    refs = list(refs)
    a_ref, w_ref = refs[:2]
    del refs[:2]
    b_ref = refs.pop(0) if gate else None
    side_refs = [refs.pop(0) for _ in range(n_side)]
    o_ref = refs.pop(0)
    side_outs = [refs.pop(0) for _ in range(n_side)]

    if precast:
        wb_ref = w_ref
    else:
        (wb_ref,) = refs

        @pl.when(pl.program_id(1) == 0)
        def _():
            _cast_weight(w_ref, wb_ref)

    for side_ref, side_out in zip(side_refs, side_outs):
        side_out[...] = side_ref[...].astype(side_out.dtype)

    acc = jnp.dot(a_ref[...], wb_ref[...], preferred_element_type=jnp.float32)
    if gate:
        acc = _sigmoid(acc + b_ref[...])
    if relu2:
        acc = jnp.square(jnp.maximum(acc, 0.0))
    o_ref[...] = acc.astype(o_ref.dtype)


def _proj(a, w_stack, layer, col0, n, *, gate_bias=None, relu2=False, side=(), tm, tn, name):
    m, k = a.shape
    nj, ni = n // tn, m // tm
    precast = w_stack.ndim == 2
    assert (w_stack.dtype == jnp.bfloat16) == precast
    row0 = 0 if precast else layer * k
    in_specs = [pl.BlockSpec((tm, k), lambda j, i: (i, 0)),
                pl.BlockSpec((pl.Element(k), pl.Element(tn)),
                             lambda j, i: (row0, pl.multiple_of(j * tn + col0, LANES)))]
    operands = [a, w_stack.reshape(-1, w_stack.shape[-1])]
    out_specs = [pl.BlockSpec((tm, tn), lambda j, i: (i, j))]
    out_shape = [jax.ShapeDtypeStruct((m, n), jnp.bfloat16)]
    if gate_bias is not None:
        in_specs.append(pl.BlockSpec((1, tn), lambda j, i: (0, j)))
        operands.append(gate_bias.reshape(1, n))
    blk0 = layer * (nj * ni)
    for s in side:
        _, s_rows, s_cols = s.shape
        rows_per = s_rows // (nj * ni)
        assert rows_per * nj * ni == s_rows and rows_per % 16 == 0
        in_specs.append(pl.BlockSpec((rows_per, s_cols), lambda j, i: (blk0 + j * ni + i, 0)))
        operands.append(s.reshape(-1, s_cols))
    for s in side:
        _, s_rows, s_cols = s.shape
        rows_per = s_rows // (nj * ni)
        out_specs.append(pl.BlockSpec((rows_per, s_cols), lambda j, i: (j * ni + i, 0)))
        out_shape.append(jax.ShapeDtypeStruct((s_rows, s_cols), jnp.bfloat16))
    return pl.pallas_call(
        functools.partial(_proj_kernel, gate=gate_bias is not None, relu2=relu2, n_side=len(side),
                          precast=precast),
        grid=(nj, ni),
        in_specs=in_specs,
        out_specs=out_specs,
        out_shape=out_shape,
        scratch_shapes=[] if precast else [pltpu.VMEM((k, tn), jnp.bfloat16)],
        compiler_params=_params(("arbitrary", "arbitrary")),
        name=name,
    )(*operands)


def _conf_branch_kernel(h_ref, w_ref, cw_ref, lng_ref, lnb_ref, side_ref, a_out, side_out,
                        wbf_scr, glu_scr, conv_scr, *, tm, tiles_per_seq):
    f32, bf16 = jnp.float32, jnp.bfloat16
    i = pl.program_id(0)
    n_slab = CONF_WIDTH // LANES

    side_out[...] = side_ref[...].astype(side_out.dtype)

    @pl.when(i == 0)
    def _():
        def cast_rows(r, carry):
            rs = pl.ds(pl.multiple_of(r * CAST_ROWS, CAST_ROWS), CAST_ROWS)
            for c in range(n_slab):
                wbf_scr[rs, (2 * c) * LANES:(2 * c + 1) * LANES] = (
                    w_ref[rs, c * LANES:(c + 1) * LANES].astype(bf16))
                wbf_scr[rs, (2 * c + 1) * LANES:(2 * c + 2) * LANES] = (
                    w_ref[rs, CONF_WIDTH + c * LANES:CONF_WIDTH + (c + 1) * LANES].astype(bf16))
            return carry
        lax.fori_loop(0, w_ref.shape[0] // CAST_ROWS, cast_rows, 0)
        glu_scr[...] = jnp.zeros_like(glu_scr)

    rows = 128
    half = rows // 2

    keep = jnp.where(i % tiles_per_seq == 0, 0.0, 1.0).astype(f32)
    for c in range(n_slab):
        glu_scr[c, 0:CONF_HALO, :] = _history(glu_scr[c, tm:tm + CONF_HALO, :], keep)

    raw = jnp.dot(h_ref[...], wbf_scr[...], preferred_element_type=f32)
    for c in range(n_slab):
        val = raw[:, (2 * c) * LANES:(2 * c + 1) * LANES]
        gate = raw[:, (2 * c + 1) * LANES:(2 * c + 2) * LANES]
        glu_scr[c, pl.ds(CONF_HALO, tm), :] = val * _sigmoid(gate)
        lanes = slice(c * LANES, (c + 1) * LANES)
        for r in range(tm // rows):
            for par in range(2):
                acc = jnp.zeros((half, LANES), f32)
                for j in range(CONF_KERNEL):
                    start = r * rows + par + CONF_HALO - (CONF_KERNEL - 1) + j
                    acc = acc + cw_ref[pl.ds(j, 1), lanes] * glu_scr[c, pl.ds(start, half, stride=2), :]
                conv_scr[c, pl.ds(r * rows + par, half, stride=2), :] = acc

    for r in range(tm // rows):
        rs = slice(r * rows, (r + 1) * rows)
        y = jnp.concatenate([conv_scr[c, rs, :] for c in range(n_slab)], axis=-1)
        mu = jnp.mean(y, axis=-1, keepdims=True)
        yc = y - mu
        var = jnp.mean(yc * yc, axis=-1, keepdims=True)
        z = yc * lax.rsqrt(var + LN_EPS) * lng_ref[...] + lnb_ref[...]
        a_out[rs, :] = (z * _sigmoid(z)).astype(a_out.dtype)


def _conf_branch(h, w_in, layer, conf_dw, ln_g, ln_b, side, *, seq, tm=BRANCH_TM):
    t, k = h.shape
    n_tiles = t // tm
    cw = CONF_WIDTH
    _, s_rows, s_cols = side.shape
    rows_per = s_rows // n_tiles
    assert rows_per * n_tiles == s_rows and rows_per % 16 == 0
    return pl.pallas_call(
        functools.partial(_conf_branch_kernel, tm=tm, tiles_per_seq=seq // tm),
        grid=(n_tiles,),
        in_specs=[pl.BlockSpec((tm, k), lambda i: (i, 0)),
                  pl.BlockSpec((None, k, 2 * cw), lambda i: (layer, 0, 0),
                               pipeline_mode=pl.Buffered(1)),
                  pl.BlockSpec((CONF_KERNEL, cw), lambda i: (0, 0)),
                  pl.BlockSpec((1, cw), lambda i: (0, 0)),
                  pl.BlockSpec((1, cw), lambda i: (0, 0)),
                  pl.BlockSpec((rows_per, s_cols), lambda i: (layer * n_tiles + i, 0))],
        out_specs=[pl.BlockSpec((tm, cw), lambda i: (i, 0)),
                   pl.BlockSpec((rows_per, s_cols), lambda i: (i, 0))],
        out_shape=[jax.ShapeDtypeStruct((t, cw), jnp.bfloat16),
                   jax.ShapeDtypeStruct((s_rows, s_cols), jnp.bfloat16)],
        scratch_shapes=[
            pltpu.VMEM((k, 2 * cw), jnp.bfloat16),
            pltpu.VMEM((cw // LANES, tm + CONF_HALO, LANES), jnp.float32),
            pltpu.VMEM((cw // LANES, tm, LANES), jnp.float32),
        ],
        compiler_params=_params(("arbitrary",)),
        name="conf_branch",
    )(h, w_in, conf_dw, ln_g.reshape(1, cw), ln_b.reshape(1, cw), side.reshape(-1, s_cols))


def _attn_branch_kernel(sinks_ref, h_ref, w_ref, c_out,
                        wbf_scr, q_scr, kt_scr, vt_scr, *, tm, tiles_per_seq):
    f32, bf16 = jnp.float32, jnp.bfloat16
    i = pl.program_id(0)
    n_blk = tm // BLOCK

    @pl.when(i == 0)
    def _():
        _cast_weight(w_ref, wbf_scr)
        kt_scr[...] = jnp.zeros_like(kt_scr)
        vt_scr[...] = jnp.zeros_like(vt_scr)

    lane_lo = lax.broadcasted_iota(jnp.int32, (BLOCK, LANES), 1) < HEAD_DIM
    qi = lax.broadcasted_iota(jnp.int32, (BLOCK, 2 * BLOCK), 0)
    kj = lax.broadcasted_iota(jnp.int32, (BLOCK, 2 * BLOCK), 1)
    band = (kj > qi) & (kj <= qi + WINDOW)
    lane_grp = kj // HEAD_DIM
    grp_mask = [jnp.where(lane_grp == g, 1.0, 0.0).astype(bf16) for g in range(GROUP)]
    contract_lanes = (((1,), (1,)), ((), ()))

    def fill_tiled(dst_scr, src, row0):
        for pair in range(KV_WIDTH // LANES):
            both = src[:, pair * LANES:(pair + 1) * LANES]
            swapped = pltpu.roll(both, HEAD_DIM, axis=1)
            even = jnp.where(lane_lo, both, swapped).astype(bf16)
            odd = jnp.where(lane_lo, swapped, both).astype(bf16)
            for rep in range(KV_WIDTH // LANES):
                dst_scr[2 * pair, pl.ds(row0, BLOCK), rep * LANES:(rep + 1) * LANES] = even
                dst_scr[2 * pair + 1, pl.ds(row0, BLOCK), rep * LANES:(rep + 1) * LANES] = odd

    def attend(b, has_prev):
        valid = band & ((kj >= BLOCK) | has_prev)
        qblk = q_scr[b * BLOCK:(b + 1) * BLOCK, :]
        for h in range(N_KV_HEADS):
            qh = qblk[:, h * GROUP * HEAD_DIM:(h + 1) * GROUP * HEAD_DIM]
            q_stack = jnp.concatenate([qh * grp_mask[g] for g in range(GROUP)], axis=0)
            s_all = lax.dot_general(q_stack, kt_scr[h, b * BLOCK:(b + 2) * BLOCK, :], contract_lanes,
                                    preferred_element_type=f32)
            probs = []
            for g in range(GROUP):
                s = jnp.where(valid, s_all[g * BLOCK:(g + 1) * BLOCK, :], -jnp.inf)
                sink = sinks_ref[h * GROUP + g]
                m = jnp.maximum(jnp.max(s, axis=-1, keepdims=True), sink)
                p = jnp.exp(s - m)
                denom = jnp.sum(p, axis=-1, keepdims=True) + jnp.exp(sink - m)
                probs.append((p * (1.0 / denom)).astype(bf16))
            o_all = jnp.dot(jnp.concatenate(probs, axis=0), vt_scr[h, b * BLOCK:(b + 2) * BLOCK, :],
                            preferred_element_type=f32)
            o = o_all[(GROUP - 1) * BLOCK:, :]
            for g in range(GROUP - 2, -1, -1):
                o = jnp.where(lane_grp == g, o_all[g * BLOCK:(g + 1) * BLOCK, :], o)
            c_out[b * BLOCK:(b + 1) * BLOCK, h * GROUP * HEAD_DIM:(h + 1) * GROUP * HEAD_DIM] = (
                o.astype(c_out.dtype))

    first = i % tiles_per_seq == 0
    keep = jnp.where(first, 0.0, 1.0).astype(f32)
    for h in range(N_KV_HEADS):
        kt_scr[h, 0:BLOCK, :] = _history(kt_scr[h, tm:tm + BLOCK, :].astype(f32), keep).astype(bf16)
        vt_scr[h, 0:BLOCK, :] = _history(vt_scr[h, tm:tm + BLOCK, :].astype(f32), keep).astype(bf16)

    kv = jnp.dot(h_ref[...], wbf_scr[:, Q_WIDTH:], preferred_element_type=f32)
    for r in range(n_blk):
        rr = slice(r * BLOCK, (r + 1) * BLOCK)
        fill_tiled(kt_scr, kv[rr, 0:KV_WIDTH], (r + 1) * BLOCK)
        fill_tiled(vt_scr, kv[rr, KV_WIDTH:], (r + 1) * BLOCK)
    q = jnp.dot(h_ref[...], wbf_scr[:, :Q_WIDTH], preferred_element_type=f32)
    q_scr[...] = (q * (HEAD_DIM ** -0.5)).astype(bf16)
    for b in range(n_blk):
        attend(b, jnp.logical_not(first) if b == 0 else True)


def _attn_branch(h, w_in, layer, sinks, *, seq, tm=BRANCH_TM):
    t, k = h.shape
    n_tiles = t // tm
    kv_scr = pltpu.VMEM((N_KV_HEADS, tm + BLOCK, GROUP * HEAD_DIM), jnp.bfloat16)
    q_scr = pltpu.VMEM((tm, Q_WIDTH), jnp.bfloat16)
    return pl.pallas_call(
        functools.partial(_attn_branch_kernel, tm=tm, tiles_per_seq=seq // tm),
        grid=(n_tiles,),
        in_specs=[pl.BlockSpec(memory_space=pltpu.SMEM),
                  pl.BlockSpec((tm, k), lambda i: (i, 0)),
                  pl.BlockSpec((pl.Element(k), pl.Element(QKV_WIDTH)), lambda i: (layer * k, OFF_Q),
                               pipeline_mode=pl.Buffered(1))],
        out_specs=pl.BlockSpec((tm, Q_WIDTH), lambda i: (i, 0)),
        out_shape=jax.ShapeDtypeStruct((t, Q_WIDTH), jnp.bfloat16),
        scratch_shapes=[pltpu.VMEM((k, QKV_WIDTH), jnp.bfloat16), q_scr, kv_scr, kv_scr],
        compiler_params=_params(("arbitrary",)),
        name="attn_branch",
    )(sinks, h, w_in.reshape(-1, w_in.shape[2]))


def _merge_out_kernel(a_ref, c_ref, bg_ref, cg_ref, bh_ref, hcg_ref, hbh_ref, sw_ref,
                      wa_ref, wb_ref, wc_ref, wm_ref, g0_ref, g1_ref, g2_ref, x_ref, gn_ref,
                      xo_ref, ho_ref, cb_scr, *, tm, tiles_per_seq):
    f32 = jnp.float32
    i = pl.program_id(0)

    keep = jnp.where(i % tiles_per_seq == 0, 0.0, 1.0).astype(f32)
    cb_scr[0:SCONV_HALO, :] = _history(hcg_ref[...].astype(f32) * hbh_ref[...].astype(f32), keep)
    cb_scr[SCONV_HALO:, :] = cg_ref[...].astype(f32) * bh_ref[...].astype(f32)
    sc = jnp.zeros((tm, SCONV_WIDTH), f32)
    for j in range(SCONV_KERNEL):
        start = SCONV_HALO - (SCONV_KERNEL - 1) + j
        sc = sc + sw_ref[pl.ds(j, 1), :] * cb_scr[pl.ds(start, tm), :]
    b_act = (bg_ref[...].astype(f32) * sc).astype(jnp.bfloat16)

    ya = jnp.dot(a_ref[...], wa_ref[...], preferred_element_type=f32)
    yb = jnp.dot(b_act, wb_ref[...], preferred_element_type=f32)
    yc = jnp.dot(c_ref[...], wc_ref[...], preferred_element_type=f32)
    merged = (g0_ref[...].astype(f32) * ya + g1_ref[...].astype(f32) * yb
              + g2_ref[...].astype(f32) * yc)
    x_new = x_ref[...] + jnp.dot(merged.astype(jnp.bfloat16), wm_ref[...], preferred_element_type=f32)
    xo_ref[...] = x_new
    ho_ref[...] = _rmsnorm_rows(x_new, gn_ref[...]).astype(ho_ref.dtype)


def _merge_out(a, c, u_s, gates, sconv_w, wa, wb, wc, wm, x, g_norm, *, seq, tm=MERGE_TM):
    t, kdim = a.shape
    d = x.shape[1]
    sw = SCONV_WIDTH
    halo_blk = tm // SCONV_HALO
    act = pl.BlockSpec((tm, kdim), lambda i: (i, 0))
    row = pl.BlockSpec((tm, d), lambda i: (i, 0))

    def weight(w):
        return pl.BlockSpec(w.shape, lambda i: (0, 0), pipeline_mode=pl.Buffered(1))

    def ucol(cb):
        return pl.BlockSpec((tm, sw), lambda i: (i, cb))

    def uhalo(cb):
        return pl.BlockSpec((SCONV_HALO, sw), lambda i: (jnp.maximum(i * halo_blk - 1, 0), cb))

    def gate(bi):
        return pl.BlockSpec((tm, d), lambda i: (i, bi))

    return pl.pallas_call(
        functools.partial(_merge_out_kernel, tm=tm, tiles_per_seq=seq // tm),
        grid=(t // tm,),
        in_specs=[act, act, ucol(0), ucol(1), ucol(2), uhalo(1), uhalo(2),
                  pl.BlockSpec((SCONV_KERNEL, sw), lambda i: (0, 0)),
                  weight(wa), weight(wb), weight(wc), weight(wm),
                  gate(0), gate(1), gate(2), row, pl.BlockSpec((1, d), lambda i: (0, 0))],
        out_specs=[row, row],
        out_shape=[jax.ShapeDtypeStruct((t, d), jnp.float32),
                   jax.ShapeDtypeStruct((t, d), jnp.bfloat16)],
        scratch_shapes=[pltpu.VMEM((tm + SCONV_HALO, sw), jnp.float32)],
        compiler_params=_params(("parallel",)),
        name="merge_out",
    )(a, c, u_s, u_s, u_s, u_s, u_s, sconv_w, wa, wb, wc, wm, gates, gates, gates, x,
      g_norm.reshape(1, d))


def _mm_res_norm_cols_kernel(a_ref, w_ref, x_ref, g_ref, xo_ref, ho_ref, *, tn):
    j = pl.program_id(1)
    last = pl.num_programs(1) - 1

    @pl.when(j < last)
    def _():
        cols = pl.ds(pl.multiple_of(j * tn, tn), tn)
        xo_ref[:, cols] = x_ref[...] + jnp.dot(a_ref[...], w_ref[...], preferred_element_type=jnp.float32)

    @pl.when(j == last)
    def _():
        n = xo_ref.shape[1]
        new = x_ref[...] + jnp.dot(a_ref[...], w_ref[...], preferred_element_type=jnp.float32)
        xo_ref[:, n - tn:] = new
        done = xo_ref[:, :n - tn]
        ms = (jnp.sum(done * done, axis=-1, keepdims=True)
              + jnp.sum(new * new, axis=-1, keepdims=True)) * (1.0 / n)
        scale = lax.rsqrt(ms + RMS_EPS)
        ho_ref[:, :n - tn] = (done * scale * g_ref[:, :n - tn]).astype(ho_ref.dtype)
        ho_ref[:, n - tn:] = (new * scale * g_ref[:, n - tn:]).astype(ho_ref.dtype)


def _mm_res_norm_cols(a, w, x, g, *, tm, tn, norm_dtype, name):
    m, kdim = a.shape
    n = w.shape[1]
    return pl.pallas_call(
        functools.partial(_mm_res_norm_cols_kernel, tn=tn),
        grid=(m // tm, n // tn),
        in_specs=[pl.BlockSpec((tm, kdim), lambda i, j: (i, 0)),
                  pl.BlockSpec((kdim, tn), lambda i, j: (0, j)),
                  pl.BlockSpec((tm, tn), lambda i, j: (i, j)),
                  pl.BlockSpec((1, n), lambda i, j: (0, 0))],
        out_specs=[pl.BlockSpec((tm, n), lambda i, j: (i, 0)),
                   pl.BlockSpec((tm, n), lambda i, j: (i, 0))],
        out_shape=[jax.ShapeDtypeStruct((m, n), jnp.float32),
                   jax.ShapeDtypeStruct((m, n), norm_dtype)],
        compiler_params=_params(("parallel", "arbitrary")),
        name=name,
    )(a, w, x, g.reshape(1, n))


def kernel(x, norm_mix_g, w_in, gate_b, conf_dw, conf_ln_g, conf_ln_b, w_conf_out,
           sconv_w, w_sconv_out, sinks, w_attn_out, w_mix_out, norm_ffn_g,
           w_up, w_down, final_g):
    bsz, seq, d = x.shape
    depth = w_in.shape[0]
    t = bsz * seq
    assert d == D_MODEL and w_in.shape[1:] == (D_MODEL, IN_WIDTH) and w_up.shape[1:] == (D_MODEL, D_FF)
    assert seq % BRANCH_TM == 0 and seq % MERGE_TM == 0 and BRANCH_TM % BLOCK == 0
    assert all(t % tm == 0 for tm in (NORM_TM, PROJ_TM, UP_TM, DOWN_TM))
    bf16 = jnp.bfloat16
    xf = x.reshape(t, d)

    h = _rmsnorm(xf, norm_mix_g[0], bf16)
    for l in range(depth):
        a_act, w_up_bf = _conf_branch(h, w_in, l, conf_dw[l], conf_ln_g[l], conf_ln_b[l], w_up, seq=seq)
        c_act = _attn_branch(h, w_in, l, sinks[l], seq=seq)
        u_s, wm_bf = _proj(h, w_in, l, OFF_B_GATE, 3 * SCONV_WIDTH, side=(w_mix_out,),
                           tm=PROJ_TM, tn=PROJ_TN, name="proj_sconv")
        gates, wa_bf, wb_bf, wc_bf = _proj(h, w_in, l, OFF_G, N_BRANCH * D_MODEL, gate_bias=gate_b[l],
                                           side=(w_conf_out, w_sconv_out, w_attn_out),
                                           tm=PROJ_TM, tn=PROJ_TN, name="proj_gates")
        xf, h2 = _merge_out(a_act, c_act, u_s, gates, sconv_w[l], wa_bf, wb_bf, wc_bf, wm_bf,
                            xf, norm_ffn_g[l], seq=seq)
        hmid, w_down_bf = _proj(h2, w_up_bf, l, 0, D_FF, relu2=True, side=(w_down,),
                                tm=UP_TM, tn=UP_TN, name="mlp_up")
        last = l == depth - 1
        g_next = final_g if last else norm_mix_g[l + 1]
        xf, h = _mm_res_norm_cols(hmid, w_down_bf, xf, g_next, tm=DOWN_TM, tn=DOWN_TN,
                                  norm_dtype=jnp.float32 if last else bf16, name="mlp_down")
    return h.reshape(bsz, seq, d)
```

```python
import functools

import jax
import jax.numpy as jnp
from jax import lax
from jax.experimental import pallas as pl
from jax.experimental.pallas import tpu as pltpu

D_MODEL = 2048
CONF_WIDTH = 1024
CONF_KERNEL = 31
SCONV_WIDTH = 1024
SCONV_KERNEL = 3
HEAD_DIM = 64
N_Q_HEADS = 16
N_KV_HEADS = 4
GROUP = N_Q_HEADS // N_KV_HEADS
Q_WIDTH = N_Q_HEADS * HEAD_DIM
KV_WIDTH = N_KV_HEADS * HEAD_DIM
WINDOW = 128
BLOCK = 128
N_BRANCH = 3
D_FF = 4 * D_MODEL
RMS_EPS = 1e-6
LN_EPS = 1e-5

OFF_A_VAL = 0
OFF_A_GATE = OFF_A_VAL + CONF_WIDTH
OFF_B_GATE = OFF_A_GATE + CONF_WIDTH
OFF_C_GATE = OFF_B_GATE + SCONV_WIDTH
OFF_B_H = OFF_C_GATE + SCONV_WIDTH
OFF_Q = OFF_B_H + SCONV_WIDTH
OFF_K = OFF_Q + Q_WIDTH
OFF_V = OFF_K + KV_WIDTH
OFF_G = OFF_V + KV_WIDTH
IN_WIDTH = OFF_G + N_BRANCH * D_MODEL
QKV_WIDTH = Q_WIDTH + 2 * KV_WIDTH
assert GROUP * HEAD_DIM == 2 * BLOCK == KV_WIDTH

LANES = 128
CONF_HALO = 32
SCONV_HALO = 16
CAST_ROWS = 256

V7X_VMEM_BYTES = 64 * 1024 * 1024
VMEM_LIMIT = V7X_VMEM_BYTES - 8 * 1024 * 1024
NORM_TM = 512
BRANCH_TM = 512
PROJ_TM, PROJ_TN = 1024, 1536
UP_TM, UP_TN = 2048, 1024
DOWN_TM, DOWN_TN, DOWN_TK = 1024, 512, 4096
MERGE_TM = 256


def _params(sem):
    return pltpu.CompilerParams(dimension_semantics=sem, vmem_limit_bytes=VMEM_LIMIT)


def _sigmoid(x):
    return 0.5 * jnp.tanh(0.5 * x) + 0.5


def _history(rows, keep):
    return jnp.where(jnp.broadcast_to(keep, rows.shape) > 0.5, rows, jnp.zeros_like(rows))


def _cast_weight(w_ref, wb_ref):
    def cast_rows(r, carry):
        rs = pl.ds(pl.multiple_of(r * CAST_ROWS, CAST_ROWS), CAST_ROWS)
        wb_ref[rs, :] = w_ref[rs, :].astype(wb_ref.dtype)
        return carry
    lax.fori_loop(0, w_ref.shape[0] // CAST_ROWS, cast_rows, 0)


def _rmsnorm_rows(x, g):
    ms = jnp.mean(x * x, axis=-1, keepdims=True)
    return x * lax.rsqrt(ms + RMS_EPS) * g


def _rmsnorm_kernel(x_ref, g_ref, o_ref):
    o_ref[...] = _rmsnorm_rows(x_ref[...], g_ref[...]).astype(o_ref.dtype)


def _rmsnorm(x, g, out_dtype, tm=NORM_TM):
    t, d = x.shape
    return pl.pallas_call(
        _rmsnorm_kernel,
        grid=(t // tm,),
        in_specs=[pl.BlockSpec((tm, d), lambda i: (i, 0)),
                  pl.BlockSpec((1, d), lambda i: (0, 0))],
        out_specs=pl.BlockSpec((tm, d), lambda i: (i, 0)),
        out_shape=jax.ShapeDtypeStruct((t, d), out_dtype),
        compiler_params=_params(("parallel",)),
        name="rmsnorm",
    )(x, g.reshape(1, d))


def _proj_kernel(*refs, gate, relu2, n_side, precast):
    refs = list(refs)
    a_ref, w_ref = refs[:2]
    del refs[:2]
    b_ref = refs.pop(0) if gate else None
    side_refs = [refs.pop(0) for _ in range(n_side)]
    o_ref = refs.pop(0)
    side_outs = [refs.pop(0) for _ in range(n_side)]

    if precast:
        wb_ref = w_ref
    else:
        (wb_ref,) = refs

        @pl.when(pl.program_id(1) == 0)
        def _():
            _cast_weight(w_ref, wb_ref)

    for side_ref, side_out in zip(side_refs, side_outs):
        side_out[...] = side_ref[...].astype(side_out.dtype)

    acc = jnp.dot(a_ref[...], wb_ref[...], preferred_element_type=jnp.float32)
    if gate:
        acc = _sigmoid(acc + b_ref[...])
    if relu2:
        acc = jnp.square(jnp.maximum(acc, 0.0))
    o_ref[...] = acc.astype(o_ref.dtype)


def _proj(a, w_stack, layer, col0, n, *, gate_bias=None, relu2=False, side=(), tm, tn, name):
    m, k = a.shape
    nj, ni = n // tn, m // tm
    precast = w_stack.ndim == 2
    assert (w_stack.dtype == jnp.bfloat16) == precast
    row0 = 0 if precast else layer * k
    in_specs = [pl.BlockSpec((tm, k), lambda j, i: (i, 0)),
                pl.BlockSpec((pl.Element(k), pl.Element(tn)),
                             lambda j, i: (row0, pl.multiple_of(j * tn + col0, LANES)))]
    operands = [a, w_stack.reshape(-1, w_stack.shape[-1])]
    out_specs = [pl.BlockSpec((tm, tn), lambda j, i: (i, j))]
    out_shape = [jax.ShapeDtypeStruct((m, n), jnp.bfloat16)]
    if gate_bias is not None:
        in_specs.append(pl.BlockSpec((1, tn), lambda j, i: (0, j)))
        operands.append(gate_bias.reshape(1, n))
    blk0 = layer * (nj * ni)
    for s in side:
        _, s_rows, s_cols = s.shape
        rows_per = s_rows // (nj * ni)
        assert rows_per * nj * ni == s_rows and rows_per % 16 == 0
        in_specs.append(pl.BlockSpec((rows_per, s_cols), lambda j, i: (blk0 + j * ni + i, 0)))
        operands.append(s.reshape(-1, s_cols))
    for s in side:
        _, s_rows, s_cols = s.shape
        rows_per = s_rows // (nj * ni)
        out_specs.append(pl.BlockSpec((rows_per, s_cols), lambda j, i: (j * ni + i, 0)))
        out_shape.append(jax.ShapeDtypeStruct((s_rows, s_cols), jnp.bfloat16))
    return pl.pallas_call(
        functools.partial(_proj_kernel, gate=gate_bias is not None, relu2=relu2, n_side=len(side),
                          precast=precast),
        grid=(nj, ni),
        in_specs=in_specs,
        out_specs=out_specs,
        out_shape=out_shape,
        scratch_shapes=[] if precast else [pltpu.VMEM((k, tn), jnp.bfloat16)],
        compiler_params=_params(("arbitrary", "arbitrary")),
        name=name,
    )(*operands)


def _conf_branch_kernel(h_ref, w_ref, cw_ref, lng_ref, lnb_ref, side_ref, a_out, side_out,
                        wbf_scr, glu_scr, conv_scr, *, tm, tiles_per_seq):
    f32, bf16 = jnp.float32, jnp.bfloat16
    i = pl.program_id(0)
    n_slab = CONF_WIDTH // LANES

    side_out[...] = side_ref[...].astype(side_out.dtype)

    @pl.when(i == 0)
    def _():
        def cast_rows(r, carry):
            rs = pl.ds(pl.multiple_of(r * CAST_ROWS, CAST_ROWS), CAST_ROWS)
            for c in range(n_slab):
                wbf_scr[rs, (2 * c) * LANES:(2 * c + 1) * LANES] = (
                    w_ref[rs, c * LANES:(c + 1) * LANES].astype(bf16))
                wbf_scr[rs, (2 * c + 1) * LANES:(2 * c + 2) * LANES] = (
                    w_ref[rs, CONF_WIDTH + c * LANES:CONF_WIDTH + (c + 1) * LANES].astype(bf16))
            return carry
        lax.fori_loop(0, w_ref.shape[0] // CAST_ROWS, cast_rows, 0)
        glu_scr[...] = jnp.zeros_like(glu_scr)

    rows = 128
    half = rows // 2

    keep = jnp.where(i % tiles_per_seq == 0, 0.0, 1.0).astype(f32)
    for c in range(n_slab):
        glu_scr[c, 0:CONF_HALO, :] = _history(glu_scr[c, tm:tm + CONF_HALO, :], keep)

    raw = jnp.dot(h_ref[...], wbf_scr[...], preferred_element_type=f32)
    for c in range(n_slab):
        val = raw[:, (2 * c) * LANES:(2 * c + 1) * LANES]
        gate = raw[:, (2 * c + 1) * LANES:(2 * c + 2) * LANES]
        glu_scr[c, pl.ds(CONF_HALO, tm), :] = val * _sigmoid(gate)
        lanes = slice(c * LANES, (c + 1) * LANES)
        for r in range(tm // rows):
            for par in range(2):
                acc = jnp.zeros((half, LANES), f32)
                for j in range(CONF_KERNEL):
                    start = r * rows + par + CONF_HALO - (CONF_KERNEL - 1) + j
                    acc = acc + cw_ref[pl.ds(j, 1), lanes] * glu_scr[c, pl.ds(start, half, stride=2), :]
                conv_scr[c, pl.ds(r * rows + par, half, stride=2), :] = acc

    for r in range(tm // rows):
        rs = slice(r * rows, (r + 1) * rows)
        y = jnp.concatenate([conv_scr[c, rs, :] for c in range(n_slab)], axis=-1)
        mu = jnp.mean(y, axis=-1, keepdims=True)
        yc = y - mu
        var = jnp.mean(yc * yc, axis=-1, keepdims=True)
        z = yc * lax.rsqrt(var + LN_EPS) * lng_ref[...] + lnb_ref[...]
        a_out[rs, :] = (z * _sigmoid(z)).astype(a_out.dtype)


def _conf_branch(h, w_in, layer, conf_dw, ln_g, ln_b, side, *, seq, tm=BRANCH_TM):
    t, k = h.shape
    n_tiles = t // tm
    cw = CONF_WIDTH
    _, s_rows, s_cols = side.shape
    rows_per = s_rows // n_tiles
    assert rows_per * n_tiles == s_rows and rows_per % 16 == 0
    return pl.pallas_call(
        functools.partial(_conf_branch_kernel, tm=tm, tiles_per_seq=seq // tm),
        grid=(n_tiles,),
        in_specs=[pl.BlockSpec((tm, k), lambda i: (i, 0)),
                  pl.BlockSpec((None, k, 2 * cw), lambda i: (layer, 0, 0),
                               pipeline_mode=pl.Buffered(1)),
                  pl.BlockSpec((CONF_KERNEL, cw), lambda i: (0, 0)),
                  pl.BlockSpec((1, cw), lambda i: (0, 0)),
                  pl.BlockSpec((1, cw), lambda i: (0, 0)),
                  pl.BlockSpec((rows_per, s_cols), lambda i: (layer * n_tiles + i, 0))],
        out_specs=[pl.BlockSpec((tm, cw), lambda i: (i, 0)),
                   pl.BlockSpec((rows_per, s_cols), lambda i: (i, 0))],
        out_shape=[jax.ShapeDtypeStruct((t, cw), jnp.bfloat16),
                   jax.ShapeDtypeStruct((s_rows, s_cols), jnp.bfloat16)],
        scratch_shapes=[
            pltpu.VMEM((k, 2 * cw), jnp.bfloat16),
            pltpu.VMEM((cw // LANES, tm + CONF_HALO, LANES), jnp.float32),
            pltpu.VMEM((cw // LANES, tm, LANES), jnp.float32),
        ],
        compiler_params=_params(("arbitrary",)),
        name="conf_branch",
    )(h, w_in, conf_dw, ln_g.reshape(1, cw), ln_b.reshape(1, cw), side.reshape(-1, s_cols))


def _attn_branch_kernel(sinks_ref, h_ref, w_ref, c_out,
                        wbf_scr, q_scr, kt_scr, vt_scr, *, tm, tiles_per_seq):
    f32, bf16 = jnp.float32, jnp.bfloat16
    i = pl.program_id(0)
    n_blk = tm // BLOCK

    @pl.when(i == 0)
    def _():
        _cast_weight(w_ref, wbf_scr)
        kt_scr[...] = jnp.zeros_like(kt_scr)
        vt_scr[...] = jnp.zeros_like(vt_scr)

    lane_lo = lax.broadcasted_iota(jnp.int32, (BLOCK, LANES), 1) < HEAD_DIM
    qi = lax.broadcasted_iota(jnp.int32, (BLOCK, 2 * BLOCK), 0)
    kj = lax.broadcasted_iota(jnp.int32, (BLOCK, 2 * BLOCK), 1)
    band = (kj > qi) & (kj <= qi + WINDOW)
    lane_grp = kj // HEAD_DIM
    grp_mask = [jnp.where(lane_grp == g, 1.0, 0.0).astype(bf16) for g in range(GROUP)]
    contract_lanes = (((1,), (1,)), ((), ()))

    def fill_tiled(dst_scr, src, row0):
        for pair in range(KV_WIDTH // LANES):
            both = src[:, pair * LANES:(pair + 1) * LANES]
            swapped = pltpu.roll(both, HEAD_DIM, axis=1)
            even = jnp.where(lane_lo, both, swapped).astype(bf16)
            odd = jnp.where(lane_lo, swapped, both).astype(bf16)
            for rep in range(KV_WIDTH // LANES):
                dst_scr[2 * pair, pl.ds(row0, BLOCK), rep * LANES:(rep + 1) * LANES] = even
                dst_scr[2 * pair + 1, pl.ds(row0, BLOCK), rep * LANES:(rep + 1) * LANES] = odd

    def attend(b, has_prev):
        valid = band & ((kj >= BLOCK) | has_prev)
        qblk = q_scr[b * BLOCK:(b + 1) * BLOCK, :]
        for h in range(N_KV_HEADS):
            qh = qblk[:, h * GROUP * HEAD_DIM:(h + 1) * GROUP * HEAD_DIM]
            q_stack = jnp.concatenate([qh * grp_mask[g] for g in range(GROUP)], axis=0)
            s_all = lax.dot_general(q_stack, kt_scr[h, b * BLOCK:(b + 2) * BLOCK, :], contract_lanes,
                                    preferred_element_type=f32)
            probs = []
            for g in range(GROUP):
                s = jnp.where(valid, s_all[g * BLOCK:(g + 1) * BLOCK, :], -jnp.inf)
                sink = sinks_ref[h * GROUP + g]
                m = jnp.maximum(jnp.max(s, axis=-1, keepdims=True), sink)
                p = jnp.exp(s - m)
                denom = jnp.sum(p, axis=-1, keepdims=True) + jnp.exp(sink - m)
                probs.append((p * (1.0 / denom)).astype(bf16))
            o_all = jnp.dot(jnp.concatenate(probs, axis=0), vt_scr[h, b * BLOCK:(b + 2) * BLOCK, :],
                            preferred_element_type=f32)
            o = o_all[(GROUP - 1) * BLOCK:, :]
            for g in range(GROUP - 2, -1, -1):
                o = jnp.where(lane_grp == g, o_all[g * BLOCK:(g + 1) * BLOCK, :], o)
            c_out[b * BLOCK:(b + 1) * BLOCK, h * GROUP * HEAD_DIM:(h + 1) * GROUP * HEAD_DIM] = (
                o.astype(c_out.dtype))

    first = i % tiles_per_seq == 0
    keep = jnp.where(first, 0.0, 1.0).astype(f32)
    for h in range(N_KV_HEADS):
        kt_scr[h, 0:BLOCK, :] = _history(kt_scr[h, tm:tm + BLOCK, :].astype(f32), keep).astype(bf16)
        vt_scr[h, 0:BLOCK, :] = _history(vt_scr[h, tm:tm + BLOCK, :].astype(f32), keep).astype(bf16)

    kv = jnp.dot(h_ref[...], wbf_scr[:, Q_WIDTH:], preferred_element_type=f32)
    for r in range(n_blk):
        rr = slice(r * BLOCK, (r + 1) * BLOCK)
        fill_tiled(kt_scr, kv[rr, 0:KV_WIDTH], (r + 1) * BLOCK)
        fill_tiled(vt_scr, kv[rr, KV_WIDTH:], (r + 1) * BLOCK)
    q = jnp.dot(h_ref[...], wbf_scr[:, :Q_WIDTH], preferred_element_type=f32)
    q_scr[...] = (q * (HEAD_DIM ** -0.5)).astype(bf16)
    for b in range(n_blk):
        attend(b, jnp.logical_not(first) if b == 0 else True)


def _attn_branch(h, w_in, layer, sinks, *, seq, tm=BRANCH_TM):
    t, k = h.shape
    n_tiles = t // tm
    kv_scr = pltpu.VMEM((N_KV_HEADS, tm + BLOCK, GROUP * HEAD_DIM), jnp.bfloat16)
    q_scr = pltpu.VMEM((tm, Q_WIDTH), jnp.bfloat16)
    return pl.pallas_call(
        functools.partial(_attn_branch_kernel, tm=tm, tiles_per_seq=seq // tm),
        grid=(n_tiles,),
        in_specs=[pl.BlockSpec(memory_space=pltpu.SMEM),
                  pl.BlockSpec((tm, k), lambda i: (i, 0)),
                  pl.BlockSpec((pl.Element(k), pl.Element(QKV_WIDTH)), lambda i: (layer * k, OFF_Q),
                               pipeline_mode=pl.Buffered(1))],
        out_specs=pl.BlockSpec((tm, Q_WIDTH), lambda i: (i, 0)),
        out_shape=jax.ShapeDtypeStruct((t, Q_WIDTH), jnp.bfloat16),
        scratch_shapes=[pltpu.VMEM((k, QKV_WIDTH), jnp.bfloat16), q_scr, kv_scr, kv_scr],
        compiler_params=_params(("arbitrary",)),
        name="attn_branch",
    )(sinks, h, w_in.reshape(-1, w_in.shape[2]))


def _merge_out_kernel(a_ref, c_ref, bg_ref, cg_ref, bh_ref, hcg_ref, hbh_ref, sw_ref,
                      wa_ref, wb_ref, wc_ref, wm_ref, g0_ref, g1_ref, g2_ref, x_ref, gn_ref,
                      xo_ref, ho_ref, cb_scr, *, tm, tiles_per_seq):
    f32 = jnp.float32
    i = pl.program_id(0)

    keep = jnp.where(i % tiles_per_seq == 0, 0.0, 1.0).astype(f32)
    cb_scr[0:SCONV_HALO, :] = _history(hcg_ref[...].astype(f32) * hbh_ref[...].astype(f32), keep)
    cb_scr[SCONV_HALO:, :] = cg_ref[...].astype(f32) * bh_ref[...].astype(f32)
    sc = jnp.zeros((tm, SCONV_WIDTH), f32)
    for j in range(SCONV_KERNEL):
        start = SCONV_HALO - (SCONV_KERNEL - 1) + j
        sc = sc + sw_ref[pl.ds(j, 1), :] * cb_scr[pl.ds(start, tm), :]
    b_act = (bg_ref[...].astype(f32) * sc).astype(jnp.bfloat16)

    ya = jnp.dot(a_ref[...], wa_ref[...], preferred_element_type=f32)
    yb = jnp.dot(b_act, wb_ref[...], preferred_element_type=f32)
    yc = jnp.dot(c_ref[...], wc_ref[...], preferred_element_type=f32)
    merged = (g0_ref[...].astype(f32) * ya + g1_ref[...].astype(f32) * yb
              + g2_ref[...].astype(f32) * yc)
    x_new = x_ref[...] + jnp.dot(merged.astype(jnp.bfloat16), wm_ref[...], preferred_element_type=f32)
    xo_ref[...] = x_new
    ho_ref[...] = _rmsnorm_rows(x_new, gn_ref[...]).astype(ho_ref.dtype)


def _merge_out(a, c, u_s, gates, sconv_w, wa, wb, wc, wm, x, g_norm, *, seq, tm=MERGE_TM):
    t, kdim = a.shape
    d = x.shape[1]
    sw = SCONV_WIDTH
    halo_blk = tm // SCONV_HALO
    act = pl.BlockSpec((tm, kdim), lambda i: (i, 0))
    row = pl.BlockSpec((tm, d), lambda i: (i, 0))

    def weight(w):
        return pl.BlockSpec(w.shape, lambda i: (0, 0), pipeline_mode=pl.Buffered(1))

    def ucol(cb):
        return pl.BlockSpec((tm, sw), lambda i: (i, cb))

    def uhalo(cb):
        return pl.BlockSpec((SCONV_HALO, sw), lambda i: (jnp.maximum(i * halo_blk - 1, 0), cb))

    def gate(bi):
        return pl.BlockSpec((tm, d), lambda i: (i, bi))

    return pl.pallas_call(
        functools.partial(_merge_out_kernel, tm=tm, tiles_per_seq=seq // tm),
        grid=(t // tm,),
        in_specs=[act, act, ucol(0), ucol(1), ucol(2), uhalo(1), uhalo(2),
                  pl.BlockSpec((SCONV_KERNEL, sw), lambda i: (0, 0)),
                  weight(wa), weight(wb), weight(wc), weight(wm),
                  gate(0), gate(1), gate(2), row, pl.BlockSpec((1, d), lambda i: (0, 0))],
        out_specs=[row, row],
        out_shape=[jax.ShapeDtypeStruct((t, d), jnp.float32),
                   jax.ShapeDtypeStruct((t, d), jnp.bfloat16)],
        scratch_shapes=[pltpu.VMEM((tm + SCONV_HALO, sw), jnp.float32)],
        compiler_params=_params(("parallel",)),
        name="merge_out",
    )(a, c, u_s, u_s, u_s, u_s, u_s, sconv_w, wa, wb, wc, wm, gates, gates, gates, x,
      g_norm.reshape(1, d))


def _mm_res_norm_cols_kernel(a_ref, w_ref, x_ref, g_ref, xo_ref, ho_ref, *, tn):
    j, k = pl.program_id(1), pl.program_id(2)
    cols = pl.ds(pl.multiple_of(j * tn, tn), tn)

    @pl.when(k == 0)
    def _():
        xo_ref[:, cols] = x_ref[...] + jnp.dot(a_ref[...], w_ref[...], preferred_element_type=jnp.float32)

    @pl.when(k > 0)
    def _():
        xo_ref[:, cols] += jnp.dot(a_ref[...], w_ref[...], preferred_element_type=jnp.float32)

    @pl.when((j == pl.num_programs(1) - 1) & (k == pl.num_programs(2) - 1))
    def _():
        ho_ref[...] = _rmsnorm_rows(xo_ref[...], g_ref[...]).astype(ho_ref.dtype)


def _mm_res_norm_cols(a, w, x, g, *, tm, tn, tk, norm_dtype, name):
    m, kdim = a.shape
    n = w.shape[1]
    return pl.pallas_call(
        functools.partial(_mm_res_norm_cols_kernel, tn=tn),
        grid=(m // tm, n // tn, kdim // tk),
        in_specs=[pl.BlockSpec((tm, tk), lambda i, j, k: (i, k)),
                  pl.BlockSpec((tk, tn), lambda i, j, k: (k, j)),
                  pl.BlockSpec((tm, tn), lambda i, j, k: (i, j)),
                  pl.BlockSpec((1, n), lambda i, j, k: (0, 0))],
        out_specs=[pl.BlockSpec((tm, n), lambda i, j, k: (i, 0)),
                   pl.BlockSpec((tm, n), lambda i, j, k: (i, 0))],
        out_shape=[jax.ShapeDtypeStruct((m, n), jnp.float32),
                   jax.ShapeDtypeStruct((m, n), norm_dtype)],
        compiler_params=_params(("parallel", "arbitrary", "arbitrary")),
        name=name,
    )(a, w, x, g.reshape(1, n))


def kernel(x, norm_mix_g, w_in, gate_b, conf_dw, conf_ln_g, conf_ln_b, w_conf_out,
           sconv_w, w_sconv_out, sinks, w_attn_out, w_mix_out, norm_ffn_g,
           w_up, w_down, final_g):
    bsz, seq, d = x.shape
    depth = w_in.shape[0]
    t = bsz * seq
    assert d == D_MODEL and w_in.shape[1:] == (D_MODEL, IN_WIDTH) and w_up.shape[1:] == (D_MODEL, D_FF)
    assert seq % BRANCH_TM == 0 and seq % MERGE_TM == 0 and BRANCH_TM % BLOCK == 0
    assert all(t % tm == 0 for tm in (NORM_TM, PROJ_TM, UP_TM, DOWN_TM))
    bf16 = jnp.bfloat16
    xf = x.reshape(t, d)

    h = _rmsnorm(xf, norm_mix_g[0], bf16)
    for l in range(depth):
        a_act, w_up_bf = _conf_branch(h, w_in, l, conf_dw[l], conf_ln_g[l], conf_ln_b[l], w_up, seq=seq)
        c_act = _attn_branch(h, w_in, l, sinks[l], seq=seq)
        u_s, wm_bf = _proj(h, w_in, l, OFF_B_GATE, 3 * SCONV_WIDTH, side=(w_mix_out,),
                           tm=PROJ_TM, tn=PROJ_TN, name="proj_sconv")
        gates, wa_bf, wb_bf, wc_bf = _proj(h, w_in, l, OFF_G, N_BRANCH * D_MODEL, gate_bias=gate_b[l],
                                           side=(w_conf_out, w_sconv_out, w_attn_out),
                                           tm=PROJ_TM, tn=PROJ_TN, name="proj_gates")
        xf, h2 = _merge_out(a_act, c_act, u_s, gates, sconv_w[l], wa_bf, wb_bf, wc_bf, wm_bf,
                            xf, norm_ffn_g[l], seq=seq)
        hmid, w_down_bf = _proj(h2, w_up_bf, l, 0, D_FF, relu2=True, side=(w_down,),
                                tm=UP_TM, tn=UP_TN, name="mlp_up")
        last = l == depth - 1
        g_next = final_g if last else norm_mix_g[l + 1]
        down_tm = DOWN_TM // 2 if last else DOWN_TM
        xf, h = _mm_res_norm_cols(hmid, w_down_bf, xf, g_next, tm=down_tm, tn=DOWN_TN, tk=DOWN_TK,
                                  norm_dtype=jnp.float32 if last else bf16, name="mlp_down")
    return h.reshape(bsz, seq, d)
```

```python
import functools

import jax
import jax.numpy as jnp
from jax import lax
from jax.experimental import pallas as pl
from jax.experimental.pallas import tpu as pltpu

D_MODEL = 2048
CONF_WIDTH = 1024
CONF_KERNEL = 31
SCONV_WIDTH = 1024
SCONV_KERNEL = 3
HEAD_DIM = 64
N_Q_HEADS = 16
N_KV_HEADS = 4
GROUP = N_Q_HEADS // N_KV_HEADS
Q_WIDTH = N_Q_HEADS * HEAD_DIM
KV_WIDTH = N_KV_HEADS * HEAD_DIM
WINDOW = 128
BLOCK = 128
N_BRANCH = 3
D_FF = 4 * D_MODEL
RMS_EPS = 1e-6
LN_EPS = 1e-5

OFF_A_VAL = 0
OFF_A_GATE = OFF_A_VAL + CONF_WIDTH
OFF_B_GATE = OFF_A_GATE + CONF_WIDTH
OFF_C_GATE = OFF_B_GATE + SCONV_WIDTH
OFF_B_H = OFF_C_GATE + SCONV_WIDTH
OFF_Q = OFF_B_H + SCONV_WIDTH
OFF_K = OFF_Q + Q_WIDTH
OFF_V = OFF_K + KV_WIDTH
OFF_G = OFF_V + KV_WIDTH
IN_WIDTH = OFF_G + N_BRANCH * D_MODEL
QKV_WIDTH = Q_WIDTH + 2 * KV_WIDTH
assert GROUP * HEAD_DIM == 2 * BLOCK == KV_WIDTH

LANES = 128
CONF_HALO = 32
SCONV_HALO = 16
CAST_ROWS = 256

V7X_VMEM_BYTES = 64 * 1024 * 1024
VMEM_LIMIT = V7X_VMEM_BYTES - 8 * 1024 * 1024
NORM_TM = 512
BRANCH_TM = 256
PROJ_TM, PROJ_TN = 1024, 1536
UP_TM, UP_TN = 2048, 1024
DOWN_TM, DOWN_TN = 512, 512
MERGE_TM = 256


def _params(sem):
    return pltpu.CompilerParams(dimension_semantics=sem, vmem_limit_bytes=VMEM_LIMIT)


def _sigmoid(x):
    return 0.5 * jnp.tanh(0.5 * x) + 0.5


def _history(rows, keep):
    return jnp.where(jnp.broadcast_to(keep, rows.shape) > 0.5, rows, jnp.zeros_like(rows))


def _cast_weight(w_ref, wb_ref):
    def cast_rows(r, carry):
        rs = pl.ds(pl.multiple_of(r * CAST_ROWS, CAST_ROWS), CAST_ROWS)
        wb_ref[rs, :] = w_ref[rs, :].astype(wb_ref.dtype)
        return carry
    lax.fori_loop(0, w_ref.shape[0] // CAST_ROWS, cast_rows, 0)


def _rmsnorm_rows(x, g):
    ms = jnp.mean(x * x, axis=-1, keepdims=True)
    return x * lax.rsqrt(ms + RMS_EPS) * g


def _rmsnorm_kernel(x_ref, g_ref, o_ref):
    o_ref[...] = _rmsnorm_rows(x_ref[...], g_ref[...]).astype(o_ref.dtype)


def _rmsnorm(x, g, out_dtype, tm=NORM_TM):
    t, d = x.shape
    return pl.pallas_call(
        _rmsnorm_kernel,
        grid=(t // tm,),
        in_specs=[pl.BlockSpec((tm, d), lambda i: (i, 0)),
                  pl.BlockSpec((1, d), lambda i: (0, 0))],
        out_specs=pl.BlockSpec((tm, d), lambda i: (i, 0)),
        out_shape=jax.ShapeDtypeStruct((t, d), out_dtype),
        compiler_params=_params(("parallel",)),
        name="rmsnorm",
    )(x, g.reshape(1, d))


def _proj_kernel(*refs, gate, relu2, n_side, precast):
    refs = list(refs)
    a_ref, w_ref = refs[:2]
    del refs[:2]
    b_ref = refs.pop(0) if gate else None
    side_refs = [refs.pop(0) for _ in range(n_side)]
    o_ref = refs.pop(0)
    side_outs = [refs.pop(0) for _ in range(n_side)]

    if precast:
        wb_ref = w_ref
    else:
        (wb_ref,) = refs

        @pl.when(pl.program_id(1) == 0)
        def _():
            _cast_weight(w_ref, wb_ref)

    for side_ref, side_out in zip(side_refs, side_outs):
        side_out[...] = side_ref[...].astype(side_out.dtype)

    acc = jnp.dot(a_ref[...], wb_ref[...], preferred_element_type=jnp.float32)
    if gate:
        acc = _sigmoid(acc + b_ref[...])
    if relu2:
        acc = jnp.square(jnp.maximum(acc, 0.0))
    o_ref[...] = acc.astype(o_ref.dtype)


def _proj(a, w_stack, layer, col0, n, *, gate_bias=None, relu2=False, side=(), tm, tn, name):
    m, k = a.shape
    nj, ni = n // tn, m // tm
    precast = w_stack.ndim == 2
    assert (w_stack.dtype == jnp.bfloat16) == precast
    row0 = 0 if precast else layer * k
    in_specs = [pl.BlockSpec((tm, k), lambda j, i: (i, 0)),
                pl.BlockSpec((pl.Element(k), pl.Element(tn)),
                             lambda j, i: (row0, pl.multiple_of(j * tn + col0, LANES)))]
    operands = [a, w_stack.reshape(-1, w_stack.shape[-1])]
    out_specs = [pl.BlockSpec((tm, tn), lambda j, i: (i, j))]
    out_shape = [jax.ShapeDtypeStruct((m, n), jnp.bfloat16)]
    if gate_bias is not None:
        in_specs.append(pl.BlockSpec((1, tn), lambda j, i: (0, j)))
        operands.append(gate_bias.reshape(1, n))
    blk0 = layer * (nj * ni)
    for s in side:
        _, s_rows, s_cols = s.shape
        rows_per = s_rows // (nj * ni)
        assert rows_per * nj * ni == s_rows and rows_per % 16 == 0
        in_specs.append(pl.BlockSpec((rows_per, s_cols), lambda j, i: (blk0 + j * ni + i, 0)))
        operands.append(s.reshape(-1, s_cols))
    for s in side:
        _, s_rows, s_cols = s.shape
        rows_per = s_rows // (nj * ni)
        out_specs.append(pl.BlockSpec((rows_per, s_cols), lambda j, i: (j * ni + i, 0)))
        out_shape.append(jax.ShapeDtypeStruct((s_rows, s_cols), jnp.bfloat16))
    return pl.pallas_call(
        functools.partial(_proj_kernel, gate=gate_bias is not None, relu2=relu2, n_side=len(side),
                          precast=precast),
        grid=(nj, ni),
        in_specs=in_specs,
        out_specs=out_specs,
        out_shape=out_shape,
        scratch_shapes=[] if precast else [pltpu.VMEM((k, tn), jnp.bfloat16)],
        compiler_params=_params(("arbitrary", "arbitrary")),
        name=name,
    )(*operands)


def _conf_init(w_ref, wbf_scr, glu_scr):
    n_slab = CONF_WIDTH // LANES

    def cast_rows(r, carry):
        rs = pl.ds(pl.multiple_of(r * CAST_ROWS, CAST_ROWS), CAST_ROWS)
        for c in range(n_slab):
            wbf_scr[rs, (2 * c) * LANES:(2 * c + 1) * LANES] = (
                w_ref[rs, c * LANES:(c + 1) * LANES].astype(wbf_scr.dtype))
            wbf_scr[rs, (2 * c + 1) * LANES:(2 * c + 2) * LANES] = (
                w_ref[rs, CONF_WIDTH + c * LANES:CONF_WIDTH + (c + 1) * LANES].astype(wbf_scr.dtype))
        return carry
    lax.fori_loop(0, w_ref.shape[0] // CAST_ROWS, cast_rows, 0)
    glu_scr[...] = jnp.zeros_like(glu_scr)


def _conf_body(h_ref, cw_ref, lng_ref, lnb_ref, a_out, wbf_scr, glu_scr, conv_scr, *, tm, tiles_per_seq):
    f32 = jnp.float32
    i = pl.program_id(0)
    n_slab = CONF_WIDTH // LANES
    rows = 128
    half = rows // 2

    keep = jnp.where(i % tiles_per_seq == 0, 0.0, 1.0).astype(f32)
    for c in range(n_slab):
        glu_scr[c, 0:CONF_HALO, :] = _history(glu_scr[c, tm:tm + CONF_HALO, :], keep)

    raw = jnp.dot(h_ref[...], wbf_scr[...], preferred_element_type=f32)
    for c in range(n_slab):
        val = raw[:, (2 * c) * LANES:(2 * c + 1) * LANES]
        gate = raw[:, (2 * c + 1) * LANES:(2 * c + 2) * LANES]
        glu_scr[c, pl.ds(CONF_HALO, tm), :] = val * _sigmoid(gate)
        lanes = slice(c * LANES, (c + 1) * LANES)
        for r in range(tm // rows):
            for par in range(2):
                acc = jnp.zeros((half, LANES), f32)
                for j in range(CONF_KERNEL):
                    start = r * rows + par + CONF_HALO - (CONF_KERNEL - 1) + j
                    acc = acc + cw_ref[pl.ds(j, 1), lanes] * glu_scr[c, pl.ds(start, half, stride=2), :]
                conv_scr[c, pl.ds(r * rows + par, half, stride=2), :] = acc

    for r in range(tm // rows):
        rs = slice(r * rows, (r + 1) * rows)
        y = jnp.concatenate([conv_scr[c, rs, :] for c in range(n_slab)], axis=-1)
        mu = jnp.mean(y, axis=-1, keepdims=True)
        yc = y - mu
        var = jnp.mean(yc * yc, axis=-1, keepdims=True)
        z = yc * lax.rsqrt(var + LN_EPS) * lng_ref[...] + lnb_ref[...]
        a_out[rs, :] = (z * _sigmoid(z)).astype(a_out.dtype)


def _attn_body(sinks_ref, h_ref, c_out, wbf_scr, q_scr, kt_scr, vt_scr, *, tm, tiles_per_seq):
    f32, bf16 = jnp.float32, jnp.bfloat16
    i = pl.program_id(0)
    n_blk = tm // BLOCK

    lane_lo = lax.broadcasted_iota(jnp.int32, (BLOCK, LANES), 1) < HEAD_DIM
    qi = lax.broadcasted_iota(jnp.int32, (BLOCK, 2 * BLOCK), 0)
    kj = lax.broadcasted_iota(jnp.int32, (BLOCK, 2 * BLOCK), 1)
    band = (kj > qi) & (kj <= qi + WINDOW)
    lane_grp = kj // HEAD_DIM
    grp_mask = [jnp.where(lane_grp == g, 1.0, 0.0).astype(bf16) for g in range(GROUP)]
    contract_lanes = (((1,), (1,)), ((), ()))

    def fill_tiled(dst_scr, src, row0):
        for pair in range(KV_WIDTH // LANES):
            both = src[:, pair * LANES:(pair + 1) * LANES]
            swapped = pltpu.roll(both, HEAD_DIM, axis=1)
            even = jnp.where(lane_lo, both, swapped).astype(bf16)
            odd = jnp.where(lane_lo, swapped, both).astype(bf16)
            for rep in range(KV_WIDTH // LANES):
                dst_scr[2 * pair, pl.ds(row0, BLOCK), rep * LANES:(rep + 1) * LANES] = even
                dst_scr[2 * pair + 1, pl.ds(row0, BLOCK), rep * LANES:(rep + 1) * LANES] = odd

    def attend(b, has_prev):
        valid = band & ((kj >= BLOCK) | has_prev)
        qblk = q_scr[b * BLOCK:(b + 1) * BLOCK, :]
        for h in range(N_KV_HEADS):
            qh = qblk[:, h * GROUP * HEAD_DIM:(h + 1) * GROUP * HEAD_DIM]
            q_stack = jnp.concatenate([qh * grp_mask[g] for g in range(GROUP)], axis=0)
            s_all = lax.dot_general(q_stack, kt_scr[h, b * BLOCK:(b + 2) * BLOCK, :], contract_lanes,
                                    preferred_element_type=f32)
            probs = []
            for g in range(GROUP):
                s = jnp.where(valid, s_all[g * BLOCK:(g + 1) * BLOCK, :], -jnp.inf)
                sink = sinks_ref[h * GROUP + g]
                m = jnp.maximum(jnp.max(s, axis=-1, keepdims=True), sink)
                p = jnp.exp(s - m)
                denom = jnp.sum(p, axis=-1, keepdims=True) + jnp.exp(sink - m)
                probs.append((p * (1.0 / denom)).astype(bf16))
            o_all = jnp.dot(jnp.concatenate(probs, axis=0), vt_scr[h, b * BLOCK:(b + 2) * BLOCK, :],
                            preferred_element_type=f32)
            o = o_all[(GROUP - 1) * BLOCK:, :]
            for g in range(GROUP - 2, -1, -1):
                o = jnp.where(lane_grp == g, o_all[g * BLOCK:(g + 1) * BLOCK, :], o)
            c_out[b * BLOCK:(b + 1) * BLOCK, h * GROUP * HEAD_DIM:(h + 1) * GROUP * HEAD_DIM] = (
                o.astype(c_out.dtype))

    first = i % tiles_per_seq == 0
    keep = jnp.where(first, 0.0, 1.0).astype(f32)
    for h in range(N_KV_HEADS):
        kt_scr[h, 0:BLOCK, :] = _history(kt_scr[h, tm:tm + BLOCK, :].astype(f32), keep).astype(bf16)
        vt_scr[h, 0:BLOCK, :] = _history(vt_scr[h, tm:tm + BLOCK, :].astype(f32), keep).astype(bf16)

    kv = jnp.dot(h_ref[...], wbf_scr[:, Q_WIDTH:], preferred_element_type=f32)
    for r in range(n_blk):
        rr = slice(r * BLOCK, (r + 1) * BLOCK)
        fill_tiled(kt_scr, kv[rr, 0:KV_WIDTH], (r + 1) * BLOCK)
        fill_tiled(vt_scr, kv[rr, KV_WIDTH:], (r + 1) * BLOCK)
    q = jnp.dot(h_ref[...], wbf_scr[:, :Q_WIDTH], preferred_element_type=f32)
    q_scr[...] = (q * (HEAD_DIM ** -0.5)).astype(bf16)
    for b in range(n_blk):
        attend(b, jnp.logical_not(first) if b == 0 else True)


def _branches_kernel(sinks_ref, h_ref, wconf_ref, cw_ref, lng_ref, lnb_ref, wqkv_ref, a_out, c_out,
                     wconf_bf, glu_scr, conv_scr, wqkv_bf, q_scr, kt_scr, vt_scr, *, tm, tiles_per_seq):
    @pl.when(pl.program_id(0) == 0)
    def _():
        _conf_init(wconf_ref, wconf_bf, glu_scr)
        _cast_weight(wqkv_ref, wqkv_bf)
        kt_scr[...] = jnp.zeros_like(kt_scr)
        vt_scr[...] = jnp.zeros_like(vt_scr)

    _conf_body(h_ref, cw_ref, lng_ref, lnb_ref, a_out, wconf_bf, glu_scr, conv_scr,
               tm=tm, tiles_per_seq=tiles_per_seq)
    _attn_body(sinks_ref, h_ref, c_out, wqkv_bf, q_scr, kt_scr, vt_scr,
               tm=tm, tiles_per_seq=tiles_per_seq)


def _branches(h, w_in, layer, conf_dw, ln_g, ln_b, sinks, *, seq, tm=BRANCH_TM):
    t, k = h.shape
    cw = CONF_WIDTH
    w2d = w_in.reshape(-1, w_in.shape[2])
    kv_scr = pltpu.VMEM((N_KV_HEADS, tm + BLOCK, GROUP * HEAD_DIM), jnp.bfloat16)

    def weight(col0, width):
        return pl.BlockSpec((pl.Element(k), pl.Element(width)), lambda i: (layer * k, col0),
                            pipeline_mode=pl.Buffered(1))

    return pl.pallas_call(
        functools.partial(_branches_kernel, tm=tm, tiles_per_seq=seq // tm),
        grid=(t // tm,),
        in_specs=[pl.BlockSpec(memory_space=pltpu.SMEM),
                  pl.BlockSpec((tm, k), lambda i: (i, 0)),
                  weight(OFF_A_VAL, 2 * cw),
                  pl.BlockSpec((CONF_KERNEL, cw), lambda i: (0, 0)),
                  pl.BlockSpec((1, cw), lambda i: (0, 0)),
                  pl.BlockSpec((1, cw), lambda i: (0, 0)),
                  weight(OFF_Q, QKV_WIDTH)],
        out_specs=[pl.BlockSpec((tm, cw), lambda i: (i, 0)),
                   pl.BlockSpec((tm, Q_WIDTH), lambda i: (i, 0))],
        out_shape=[jax.ShapeDtypeStruct((t, cw), jnp.bfloat16),
                   jax.ShapeDtypeStruct((t, Q_WIDTH), jnp.bfloat16)],
        scratch_shapes=[pltpu.VMEM((k, 2 * cw), jnp.bfloat16),
                        pltpu.VMEM((cw // LANES, tm + CONF_HALO, LANES), jnp.float32),
                        pltpu.VMEM((cw // LANES, tm, LANES), jnp.float32),
                        pltpu.VMEM((k, QKV_WIDTH), jnp.bfloat16),
                        pltpu.VMEM((tm, Q_WIDTH), jnp.bfloat16), kv_scr, kv_scr],
        compiler_params=_params(("arbitrary",)),
        name="branches",
    )(sinks, h, w2d, conf_dw, ln_g.reshape(1, cw), ln_b.reshape(1, cw), w2d)


def _merge_out_kernel(a_ref, c_ref, bg_ref, cg_ref, bh_ref, hcg_ref, hbh_ref, sw_ref,
                      wa_ref, wb_ref, wc_ref, wm_ref, g0_ref, g1_ref, g2_ref, x_ref, gn_ref, side_ref,
                      xo_ref, ho_ref, side_out, cb_scr, *, tm, tiles_per_seq):
    f32 = jnp.float32
    i = pl.program_id(0)

    side_out[...] = side_ref[...].astype(side_out.dtype)

    keep = jnp.where(i % tiles_per_seq == 0, 0.0, 1.0).astype(f32)
    cb_scr[0:SCONV_HALO, :] = _history(hcg_ref[...].astype(f32) * hbh_ref[...].astype(f32), keep)
    cb_scr[SCONV_HALO:, :] = cg_ref[...].astype(f32) * bh_ref[...].astype(f32)
    sc = jnp.zeros((tm, SCONV_WIDTH), f32)
    for j in range(SCONV_KERNEL):
        start = SCONV_HALO - (SCONV_KERNEL - 1) + j
        sc = sc + sw_ref[pl.ds(j, 1), :] * cb_scr[pl.ds(start, tm), :]
    b_act = (bg_ref[...].astype(f32) * sc).astype(jnp.bfloat16)

    ya = jnp.dot(a_ref[...], wa_ref[...], preferred_element_type=f32)
    yb = jnp.dot(b_act, wb_ref[...], preferred_element_type=f32)
    yc = jnp.dot(c_ref[...], wc_ref[...], preferred_element_type=f32)
    merged = (g0_ref[...].astype(f32) * ya + g1_ref[...].astype(f32) * yb
              + g2_ref[...].astype(f32) * yc)
    x_new = x_ref[...] + jnp.dot(merged.astype(jnp.bfloat16), wm_ref[...], preferred_element_type=f32)
    xo_ref[...] = x_new
    ho_ref[...] = _rmsnorm_rows(x_new, gn_ref[...]).astype(ho_ref.dtype)


def _merge_out(a, c, u_s, gates, sconv_w, wa, wb, wc, wm, x, g_norm, side, layer, *, seq, tm=MERGE_TM):
    t, kdim = a.shape
    d = x.shape[1]
    sw = SCONV_WIDTH
    halo_blk = tm // SCONV_HALO
    n_tiles = t // tm
    _, s_rows, s_cols = side.shape
    rows_per = s_rows // n_tiles
    assert rows_per * n_tiles == s_rows and rows_per % 16 == 0
    act = pl.BlockSpec((tm, kdim), lambda i: (i, 0))
    row = pl.BlockSpec((tm, d), lambda i: (i, 0))

    def weight(w):
        return pl.BlockSpec(w.shape, lambda i: (0, 0), pipeline_mode=pl.Buffered(1))

    def ucol(cb):
        return pl.BlockSpec((tm, sw), lambda i: (i, cb))

    def uhalo(cb):
        return pl.BlockSpec((SCONV_HALO, sw), lambda i: (jnp.maximum(i * halo_blk - 1, 0), cb))

    def gate(bi):
        return pl.BlockSpec((tm, d), lambda i: (i, bi))

    return pl.pallas_call(
        functools.partial(_merge_out_kernel, tm=tm, tiles_per_seq=seq // tm),
        grid=(t // tm,),
        in_specs=[act, act, ucol(0), ucol(1), ucol(2), uhalo(1), uhalo(2),
                  pl.BlockSpec((SCONV_KERNEL, sw), lambda i: (0, 0)),
                  weight(wa), weight(wb), weight(wc), weight(wm),
                  gate(0), gate(1), gate(2), row, pl.BlockSpec((1, d), lambda i: (0, 0)),
                  pl.BlockSpec((rows_per, s_cols), lambda i: (layer * n_tiles + i, 0))],
        out_specs=[row, row, pl.BlockSpec((rows_per, s_cols), lambda i: (i, 0))],
        out_shape=[jax.ShapeDtypeStruct((t, d), jnp.float32),
                   jax.ShapeDtypeStruct((t, d), jnp.bfloat16),
                   jax.ShapeDtypeStruct((s_rows, s_cols), jnp.bfloat16)],
        scratch_shapes=[pltpu.VMEM((tm + SCONV_HALO, sw), jnp.float32)],
        compiler_params=_params(("parallel",)),
        name="merge_out",
    )(a, c, u_s, u_s, u_s, u_s, u_s, sconv_w, wa, wb, wc, wm, gates, gates, gates, x,
      g_norm.reshape(1, d), side.reshape(-1, s_cols))


def _mm_res_norm_cols_kernel(a_ref, w_ref, x_ref, g_ref, xo_ref, ho_ref, *, tn):
    j = pl.program_id(1)
    last = pl.num_programs(1) - 1

    @pl.when(j < last)
    def _():
        cols = pl.ds(pl.multiple_of(j * tn, tn), tn)
        xo_ref[:, cols] = x_ref[...] + jnp.dot(a_ref[...], w_ref[...], preferred_element_type=jnp.float32)

    @pl.when(j == last)
    def _():
        n = xo_ref.shape[1]
        new = x_ref[...] + jnp.dot(a_ref[...], w_ref[...], preferred_element_type=jnp.float32)
        xo_ref[:, n - tn:] = new
        done = xo_ref[:, :n - tn]
        ms = (jnp.sum(done * done, axis=-1, keepdims=True)
              + jnp.sum(new * new, axis=-1, keepdims=True)) * (1.0 / n)
        scale = lax.rsqrt(ms + RMS_EPS)
        ho_ref[:, :n - tn] = (done * scale * g_ref[:, :n - tn]).astype(ho_ref.dtype)
        ho_ref[:, n - tn:] = (new * scale * g_ref[:, n - tn:]).astype(ho_ref.dtype)


def _mm_res_norm_cols(a, w, x, g, *, tm, tn, norm_dtype, name):
    m, kdim = a.shape
    n = w.shape[1]
    return pl.pallas_call(
        functools.partial(_mm_res_norm_cols_kernel, tn=tn),
        grid=(m // tm, n // tn),
        in_specs=[pl.BlockSpec((tm, kdim), lambda i, j: (i, 0)),
                  pl.BlockSpec((kdim, tn), lambda i, j: (0, j)),
                  pl.BlockSpec((tm, tn), lambda i, j: (i, j)),
                  pl.BlockSpec((1, n), lambda i, j: (0, 0))],
        out_specs=[pl.BlockSpec((tm, n), lambda i, j: (i, 0)),
                   pl.BlockSpec((tm, n), lambda i, j: (i, 0))],
        out_shape=[jax.ShapeDtypeStruct((m, n), jnp.float32),
                   jax.ShapeDtypeStruct((m, n), norm_dtype)],
        compiler_params=_params(("parallel", "arbitrary")),
        name=name,
    )(a, w, x, g.reshape(1, n))


def kernel(x, norm_mix_g, w_in, gate_b, conf_dw, conf_ln_g, conf_ln_b, w_conf_out,
           sconv_w, w_sconv_out, sinks, w_attn_out, w_mix_out, norm_ffn_g,
           w_up, w_down, final_g):
    bsz, seq, d = x.shape
    depth = w_in.shape[0]
    t = bsz * seq
    assert d == D_MODEL and w_in.shape[1:] == (D_MODEL, IN_WIDTH) and w_up.shape[1:] == (D_MODEL, D_FF)
    assert seq % BRANCH_TM == 0 and seq % MERGE_TM == 0 and BRANCH_TM % BLOCK == 0
    assert all(t % tm == 0 for tm in (NORM_TM, PROJ_TM, UP_TM, DOWN_TM))
    bf16 = jnp.bfloat16
    xf = x.reshape(t, d)

    h = _rmsnorm(xf, norm_mix_g[0], bf16)
    for l in range(depth):
        a_act, c_act = _branches(h, w_in, l, conf_dw[l], conf_ln_g[l], conf_ln_b[l], sinks[l], seq=seq)
        u_s, wm_bf = _proj(h, w_in, l, OFF_B_GATE, 3 * SCONV_WIDTH, side=(w_mix_out,),
                           tm=PROJ_TM, tn=PROJ_TN, name="proj_sconv")
        gates, wa_bf, wb_bf, wc_bf = _proj(h, w_in, l, OFF_G, N_BRANCH * D_MODEL, gate_bias=gate_b[l],
                                           side=(w_conf_out, w_sconv_out, w_attn_out),
                                           tm=PROJ_TM, tn=PROJ_TN, name="proj_gates")
        xf, h2, w_up_bf = _merge_out(a_act, c_act, u_s, gates, sconv_w[l], wa_bf, wb_bf, wc_bf, wm_bf,
                                     xf, norm_ffn_g[l], w_up, l, seq=seq)
        hmid, w_down_bf = _proj(h2, w_up_bf, l, 0, D_FF, relu2=True, side=(w_down,),
                                tm=UP_TM, tn=UP_TN, name="mlp_up")
        last = l == depth - 1
        g_next = final_g if last else norm_mix_g[l + 1]
        xf, h = _mm_res_norm_cols(hmid, w_down_bf, xf, g_next, tm=DOWN_TM, tn=DOWN_TN,
                                  norm_dtype=jnp.float32 if last else bf16, name="mlp_down")
    return h.reshape(bsz, seq, d)
```
